```python
import jax, jax.numpy as jnp
from jax import lax
import numpy as np

D_MODEL = 1024
BATCH = 4
SEQ = 4096
DEPTH = 1
DEC_BATCH = 128
DEC_SEQ = 4
PAST_LEN = 16384
PAGE_SIZE = 128

D_CONV = 1024
CONV_W = 3
N_HEADS = 16
QK_NOPE = 128
QK_ROPE = 64
V_DIM = 128
Q_LORA = 256
KV_LORA = 256
ROPE_THETA = 10000.0
D_FF = 4 * D_MODEL
Q_BLOCK = 128
NORM_EPS = 1e-5
ATTN_SCALE = (QK_NOPE + QK_ROPE) ** -0.5
DEEPNORM_ALPHA = (2 * DEPTH) ** 0.25
DEEPNORM_BETA = (8 * DEPTH) ** -0.25
IN_SIZES = (D_CONV, D_CONV, D_CONV, Q_LORA, KV_LORA, QK_ROPE, D_MODEL, D_MODEL)
IN_SPLITS = tuple(int(v) for v in np.cumsum(IN_SIZES)[:-1])
D_IN = int(sum(IN_SIZES))
MASK_VALUE = -1e30

kernel_name = 'hybrid_shortconv_mla_deepnorm_step'


def layer_norm(x, g, b):
    xf = x.astype(jnp.float32)
    mu = jnp.mean(xf, axis=-1, keepdims=True)
    var = jnp.mean(jnp.square(xf - mu), axis=-1, keepdims=True)
    return ((xf - mu) * lax.rsqrt(var + NORM_EPS) * g + b).astype(x.dtype)


def rms_norm(x, g):
    xf = x.astype(jnp.float32)
    ms = jnp.mean(jnp.square(xf), axis=-1, keepdims=True)
    return (xf * lax.rsqrt(ms + NORM_EPS) * g).astype(x.dtype)


def rope_angles(pos):
    freqs = ROPE_THETA ** (-jnp.arange(0, QK_ROPE, 2, dtype=jnp.float32) / QK_ROPE)
    ang = pos.astype(jnp.float32)[:, None] * freqs[None, :]
    return jnp.cos(ang), jnp.sin(ang)


def apply_rope(x, cos, sin):
    x1, x2 = jnp.split(x.astype(jnp.float32), 2, axis=-1)
    return jnp.concatenate([x1 * cos - x2 * sin, x1 * sin + x2 * cos], axis=-1).astype(x.dtype)


def causal_short_conv(u, prev, w):
    t = u.shape[1]
    up = jnp.concatenate([prev, u], axis=1)
    y = w[0] * up[:, 0:t]
    for k in range(1, CONV_W):
        y = y + w[k] * up[:, k:k + t]
    return y, up[:, -(CONV_W - 1):]


def latent_attention(q_lat, q_rope, k_lat, k_rope, q_pos, k_pos):
    s = (jnp.einsum('bqhc,bkc->bhqk', q_lat, k_lat)
         + jnp.einsum('bqhr,bkr->bhqk', q_rope, k_rope)).astype(jnp.float32) * ATTN_SCALE
    s = jnp.where(k_pos[None, :] <= q_pos[:, None], s, MASK_VALUE)
    p = jax.nn.softmax(s, axis=-1).astype(k_lat.dtype)
    return jnp.einsum('bhqk,bkc->bqhc', p, k_lat)


def prompt_attend(q_lat, q_rope, lat, k_rope):
    bsz, seq = lat.shape[0], lat.shape[1]
    k_pos = jnp.arange(seq)

    def block(i):
        start = i * Q_BLOCK
        ql = lax.dynamic_slice_in_dim(q_lat, start, Q_BLOCK, axis=1)
        qr = lax.dynamic_slice_in_dim(q_rope, start, Q_BLOCK, axis=1)
        q_pos = start + jnp.arange(Q_BLOCK)
        return latent_attention(ql, qr, lat, k_rope, q_pos, k_pos)

    o = lax.map(block, jnp.arange(seq // Q_BLOCK))
    return jnp.transpose(o, (1, 0, 2, 3, 4)).reshape(bsz, seq, N_HEADS, KV_LORA)


def make_sample_attend(cache_lat_l, cache_rope_l, page_table):
    past = page_table.shape[1] * PAGE_SIZE

    def attend(q_lat, q_rope, lat, k_rope):
        t = lat.shape[1]
        q_pos = past + jnp.arange(t)
        k_pos = jnp.arange(past + t)

        def one(args):
            ql, qr, nl, nr, pt = args
            past_l = cache_lat_l[pt].reshape(past, KV_LORA)
            past_r = cache_rope_l[pt].reshape(past, QK_ROPE)
            kl = jnp.concatenate([past_l, nl], axis=0)
            kr = jnp.concatenate([past_r, nr], axis=0)
            return latent_attention(ql[None], qr[None], kl[None], kr[None], q_pos, k_pos)[0]

        return lax.map(one, (q_lat, q_rope, lat, k_rope, page_table))

    return attend


def hybrid_layer(x, pos, conv_prev, attend, w_in, conv_w, q_norm_g, w_qb, kv_norm_g, w_kb, w_vb,
                 w_conv_out, w_attn_out, w_mix_out, ln1_g, ln1_b, w_ff1, w_ff2, ln2_g, ln2_b):
    bsz, t, _ = x.shape
    proj = jnp.einsum('btd,de->bte', x, w_in)
    b_gate, c_gate, h, c_q, c_kv, k_r, g_conv, g_attn = jnp.split(proj, IN_SPLITS, axis=-1)
    v, conv_new = causal_short_conv(c_gate * h, conv_prev, conv_w)
    a = jnp.einsum('btc,cd->btd', b_gate * v, w_conv_out)
    cos, sin = rope_angles(pos)
    q = jnp.einsum('btr,re->bte', rms_norm(c_q, q_norm_g), w_qb).reshape(bsz, t, N_HEADS, QK_NOPE + QK_ROPE)
    q_nope, q_rope = jnp.split(q, [QK_NOPE], axis=-1)
    q_rope = apply_rope(q_rope, cos[:, None, :], sin[:, None, :])
    q_lat = jnp.einsum('bthd,chd->bthc', q_nope, w_kb)
    lat = rms_norm(c_kv, kv_norm_g)
    k_rope = apply_rope(k_r, cos, sin)
    o_lat = attend(q_lat, q_rope, lat, k_rope)
    o = jnp.einsum('bthc,chd->bthd', o_lat, w_vb).reshape(bsz, t, N_HEADS * V_DIM)
    m = jnp.einsum('bte,ed->btd', o, w_attn_out)
    z = jnp.einsum('bte,ed->btd', jax.nn.sigmoid(g_conv) * a + jax.nn.sigmoid(g_attn) * m, w_mix_out)
    h1 = layer_norm(DEEPNORM_ALPHA * x + z, ln1_g, ln1_b)
    f = jnp.einsum('btf,fd->btd', jnp.square(jax.nn.relu(jnp.einsum('btd,df->btf', h1, w_ff1))), w_ff2)
    y = layer_norm(DEEPNORM_ALPHA * h1 + f, ln2_g, ln2_b)
    return y, lat, k_rope, conv_new


def setup_inputs(seed: int = 0) -> dict:
    key = jax.random.key(seed)
    ks = jax.random.split(key, 24)
    n_pages = PAST_LEN // PAGE_SIZE
    n_pool = (DEC_BATCH * n_pages * 5) // 4
    nrm = jax.random.normal
    page_table = jax.random.permutation(ks[0], n_pool)[:DEC_BATCH * n_pages].reshape(DEC_BATCH, n_pages).astype(jnp.int32)
    return {
        'x_prompt': nrm(ks[1], (BATCH, SEQ, D_MODEL), jnp.float32),
        'x_sample': nrm(ks[2], (DEC_BATCH, DEC_SEQ, D_MODEL), jnp.float32),
        'cache_latent': nrm(ks[3], (DEPTH, n_pool, PAGE_SIZE, KV_LORA), jnp.float32),
        'cache_krope': nrm(ks[4], (DEPTH, n_pool, PAGE_SIZE, QK_ROPE), jnp.float32),
        'state_conv': nrm(ks[5], (DEPTH, DEC_BATCH, CONV_W - 1, D_CONV), jnp.float32),
        'page_table': page_table,
        'w_in': nrm(ks[6], (DEPTH, D_MODEL, D_IN), jnp.float32) * D_MODEL ** -0.5,
        'conv_w': nrm(ks[7], (DEPTH, CONV_W, D_CONV), jnp.float32) * CONV_W ** -0.5,
        'q_norm_g': 1.0 + 0.01 * nrm(ks[8], (DEPTH, Q_LORA), jnp.float32),
        'w_qb': nrm(ks[9], (DEPTH, Q_LORA, N_HEADS * (QK_NOPE + QK_ROPE)), jnp.float32) * Q_LORA ** -0.5,
        'kv_norm_g': 1.0 + 0.01 * nrm(ks[10], (DEPTH, KV_LORA), jnp.float32),
        'w_kb': nrm(ks[11], (DEPTH, KV_LORA, N_HEADS, QK_NOPE), jnp.float32) * KV_LORA ** -0.5,
        'w_vb': nrm(ks[12], (DEPTH, KV_LORA, N_HEADS, V_DIM), jnp.float32) * (DEEPNORM_BETA * KV_LORA ** -0.5),
        'w_conv_out': nrm(ks[13], (DEPTH, D_CONV, D_MODEL), jnp.float32) * (DEEPNORM_BETA * D_CONV ** -0.5),
        'w_attn_out': nrm(ks[14], (DEPTH, N_HEADS * V_DIM, D_MODEL), jnp.float32) * (DEEPNORM_BETA * (N_HEADS * V_DIM) ** -0.5),
        'w_mix_out': nrm(ks[15], (DEPTH, D_MODEL, D_MODEL), jnp.float32) * (DEEPNORM_BETA * D_MODEL ** -0.5),
        'ln1_g': 1.0 + 0.01 * nrm(ks[16], (DEPTH, D_MODEL), jnp.float32),
        'ln1_b': 0.01 * nrm(ks[17], (DEPTH, D_MODEL), jnp.float32),
        'w_ff1': nrm(ks[18], (DEPTH, D_MODEL, D_FF), jnp.float32) * D_MODEL ** -0.5,
        'w_ff2': nrm(ks[19], (DEPTH, D_FF, D_MODEL), jnp.float32) * (DEEPNORM_BETA * D_FF ** -0.5),
        'ln2_g': 1.0 + 0.01 * nrm(ks[20], (DEPTH, D_MODEL), jnp.float32),
        'ln2_b': 0.01 * nrm(ks[21], (DEPTH, D_MODEL), jnp.float32),
    }


def reference(x_prompt, x_sample, cache_latent, cache_krope, state_conv, page_table,
              w_in, conv_w, q_norm_g, w_qb, kv_norm_g, w_kb, w_vb, w_conv_out, w_attn_out, w_mix_out,
              ln1_g, ln1_b, w_ff1, w_ff2, ln2_g, ln2_b):
    past = page_table.shape[1] * PAGE_SIZE
    pos_prompt = jnp.arange(x_prompt.shape[1])
    pos_sample = past + jnp.arange(x_sample.shape[1])
    conv_zero = jnp.zeros((x_prompt.shape[0], CONV_W - 1, D_CONV), x_prompt.dtype)
    y_p, y_s = x_prompt, x_sample
    lat_p, rope_p, conv_p, lat_s, rope_s, conv_s = [], [], [], [], [], []
    for l in range(DEPTH):
        lw = (w_in[l], conv_w[l], q_norm_g[l], w_qb[l], kv_norm_g[l], w_kb[l], w_vb[l],
              w_conv_out[l], w_attn_out[l], w_mix_out[l], ln1_g[l], ln1_b[l], w_ff1[l], w_ff2[l], ln2_g[l], ln2_b[l])
        y_p, lp, rp, cp = hybrid_layer(y_p, pos_prompt, conv_zero, prompt_attend, *lw)
        attend_s = make_sample_attend(cache_latent[l], cache_krope[l], page_table)
        y_s, ls, rs, cs = hybrid_layer(y_s, pos_sample, state_conv[l], attend_s, *lw)
        lat_p.append(lp); rope_p.append(rp); conv_p.append(cp)
        lat_s.append(ls); rope_s.append(rs); conv_s.append(cs)
    new_latent_prompt = jnp.stack(lat_p)
    new_krope_prompt = jnp.stack(rope_p)
    new_conv_prompt = jnp.stack(conv_p)
    new_latent_sample = jnp.stack(lat_s)
    new_krope_sample = jnp.stack(rope_s)
    new_conv_sample = jnp.stack(conv_s)
    return (y_p, y_s, new_latent_prompt, new_krope_prompt, new_conv_prompt, new_latent_sample, new_krope_sample, new_conv_sample)
```

```python
import functools

import jax
import jax.numpy as jnp
import numpy as np
from jax import lax
from jax.experimental import pallas as pl
from jax.experimental.pallas import tpu as pltpu

F32 = jnp.float32
BF16 = jnp.bfloat16

D_MODEL = 1024
D_CONV = 1024
CONV_W = 3
N_HEADS = 16
QK_NOPE = 128
QK_ROPE = 64
V_DIM = 128
Q_LORA = 256
KV_LORA = 256
ROPE_THETA = 10000.0
D_FF = 4 * D_MODEL
NORM_EPS = 1e-5
PAGE_SIZE = 128
MASK_VALUE = -1e30
ATTN_SCALE = (QK_NOPE + QK_ROPE) ** -0.5

LANES = 128
HEAD_PAD = 2 * LANES
SMALL_W = Q_LORA + KV_LORA + LANES
VMEM_LIMIT = 56 * 1024 * 1024

TOKEN_TILE = 512
ATTN_Q_TILE = 512
ATTN_K_TILE = 512
PAGES_PER_STEP = 16
NEW_KEY_PAD = 16


def _dot(a, b):
    return jnp.dot(a, b, preferred_element_type=F32)


def _dot_nt(a, b):
    return lax.dot_general(a, b, (((1,), (1,)), ((), ())), preferred_element_type=F32)


def _const_spec(shape):
    zeros = (0,) * len(shape)
    return pl.BlockSpec(shape, lambda *_: zeros, pipeline_mode=pl.Buffered(1))


def _params(n_grid_dims):
    return pltpu.CompilerParams(
        dimension_semantics=("arbitrary",) * n_grid_dims,
        vmem_limit_bytes=VMEM_LIMIT,
    )


def _rms_norm(x, g):
    ms = jnp.mean(jnp.square(x), axis=-1, keepdims=True)
    return x * lax.rsqrt(ms + NORM_EPS) * g


def _layer_norm(x, g, b):
    mu = jnp.mean(x, axis=-1, keepdims=True)
    xc = x - mu
    var = jnp.mean(jnp.square(xc), axis=-1, keepdims=True)
    return xc * lax.rsqrt(var + NORM_EPS) * g + b


def _rope_padded(r, cos, sin_lo, sin_hi):
    return (r * cos
            + pltpu.roll(r, LANES - QK_ROPE // 2, axis=1) * sin_lo
            + pltpu.roll(r, QK_ROPE // 2, axis=1) * sin_hi)


def _conv_gate_kernel(x_ref, state_ref, w_in_ref, conv_w_ref, w_co_ref,
                      ga_ref, sg_ref, conv_new_ref, u_scr, *, sample_rows):
    tm = x_ref.shape[0]
    xb = x_ref[...].astype(BF16)
    c = D_CONV
    u = _dot(xb, w_in_ref[:, c:2 * c]) * _dot(xb, w_in_ref[:, 2 * c:3 * c])
    w0 = conv_w_ref[0:1, :]
    w1 = conv_w_ref[1:2, :]
    w2 = conv_w_ref[2:3, :]
    if sample_rows is None:
        @pl.when(pl.program_id(1) == 0)
        def _():
            u_scr[0:6, :] = jnp.zeros((6, c), F32)
            u_scr[6:8, :] = state_ref[...]
        u_scr[8:8 + tm, :] = u
        v = w2 * u + w1 * u_scr[7:7 + tm, :] + w0 * u_scr[6:6 + tm, :]
        tail = u_scr[tm:tm + 8, :]
        u_scr[0:8, :] = tail
        conv_new_ref[...] = tail[6:8, :]
    else:
        s = sample_rows
        u_scr[0:2 * s, :] = state_ref[...]
        u_scr[2 * s:2 * s + tm, :] = u
        v = w2 * u + w1 * u_scr[s:s + tm, :] + w0 * u_scr[0:tm, :]
        conv_new_ref[...] = u_scr[tm:tm + 2 * s, :]
    a_in = (_dot(xb, w_in_ref[:, 0:c]) * v).astype(BF16)
    a = _dot(a_in, w_co_ref[...])
    ga_ref[...] = jax.nn.sigmoid(_dot(xb, w_in_ref[:, 3 * c:3 * c + D_MODEL])) * a
    sg_ref[...] = jax.nn.sigmoid(_dot(xb, w_in_ref[:, 3 * c + D_MODEL:3 * c + 2 * D_MODEL]))


def _conv_gate(x, state, w_in_cg, conv_w, w_co, *, sample_rows):
    g, t, d = x.shape
    tm = min(TOKEN_TILE, t)
    n_state = state.shape[1]
    scr_rows = 8 + tm if sample_rows is None else 2 * sample_rows + tm
    tok = lambda w: pl.BlockSpec((None, tm, w), lambda b, i: (b, i, 0))
    return pl.pallas_call(
        functools.partial(_conv_gate_kernel, sample_rows=sample_rows),
        grid=(g, t // tm),
        in_specs=[tok(d),
                  pl.BlockSpec((None, n_state, D_CONV), lambda b, i: (b, 0, 0)),
                  _const_spec(w_in_cg.shape), _const_spec(conv_w.shape), _const_spec(w_co.shape)],
        out_specs=[tok(D_MODEL), tok(D_MODEL),
                   pl.BlockSpec((None, n_state, D_CONV), lambda b, i: (b, 0, 0))],
        out_shape=[jax.ShapeDtypeStruct((g, t, D_MODEL), F32),
                   jax.ShapeDtypeStruct((g, t, D_MODEL), F32),
                   jax.ShapeDtypeStruct((g, n_state, D_CONV), F32)],
        scratch_shapes=[pltpu.VMEM((scr_rows, D_CONV), F32)],
        compiler_params=_params(2),
        name="conv_gate",
    )(x, state, w_in_cg, conv_w, w_co)


def _attn_front_kernel(x_ref, cos_ref, sin_lo_ref, sin_hi_ref, w_small_ref, qg_ref, kvg_ref,
                       w_qb_ref, w_kb_ref, w_vb_ref,
                       q_ref, k_ref, v_ref, lat_ref, krope_ref):
    xb = x_ref[...].astype(BF16)
    small = _dot(xb, w_small_ref[...])
    cos = cos_ref[...]
    sin_lo = sin_lo_ref[...]
    sin_hi = sin_hi_ref[...]

    qn = _rms_norm(small[:, 0:Q_LORA], qg_ref[...]).astype(BF16)
    q = _dot(qn, w_qb_ref[...])
    for h in range(N_HEADS):
        lo = h * HEAD_PAD
        q_ref[:, lo:lo + LANES] = q[:, lo:lo + LANES].astype(BF16)
        q_ref[:, lo + LANES:lo + HEAD_PAD] = _rope_padded(
            q[:, lo + LANES:lo + HEAD_PAD], cos, sin_lo, sin_hi).astype(BF16)

    lat = _rms_norm(small[:, Q_LORA:Q_LORA + KV_LORA], kvg_ref[...])
    lat_ref[...] = lat
    latb = lat.astype(BF16)
    krope = _rope_padded(small[:, Q_LORA + KV_LORA:SMALL_W], cos, sin_lo, sin_hi)
    krope_ref[...] = krope[:, 0:QK_ROPE]
    kropeb = krope.astype(BF16)
    kn = _dot(latb, w_kb_ref[...]).astype(BF16)
    for h in range(N_HEADS):
        lo = h * HEAD_PAD
        k_ref[:, lo:lo + LANES] = kn[:, h * QK_NOPE:(h + 1) * QK_NOPE]
        k_ref[:, lo + LANES:lo + HEAD_PAD] = kropeb
    v_ref[...] = _dot(latb, w_vb_ref[...]).astype(BF16)


def _attn_front(x, tables, w_small, qg, kvg, w_qb, w_kb, w_vb):
    g, t, d = x.shape
    tm = min(TOKEN_TILE, t)
    tok = lambda w: pl.BlockSpec((None, tm, w), lambda b, i: (b, i, 0))
    tab = pl.BlockSpec((tm, LANES), lambda b, i: (i, 0))
    return pl.pallas_call(
        _attn_front_kernel,
        grid=(g, t // tm),
        in_specs=[tok(d), tab, tab, tab,
                  _const_spec(w_small.shape), _const_spec(qg.shape), _const_spec(kvg.shape),
                  _const_spec(w_qb.shape), _const_spec(w_kb.shape), _const_spec(w_vb.shape)],
        out_specs=[tok(N_HEADS * HEAD_PAD), tok(N_HEADS * HEAD_PAD), tok(N_HEADS * V_DIM),
                   tok(KV_LORA), tok(QK_ROPE)],
        out_shape=[jax.ShapeDtypeStruct((g, t, N_HEADS * HEAD_PAD), BF16),
                   jax.ShapeDtypeStruct((g, t, N_HEADS * HEAD_PAD), BF16),
                   jax.ShapeDtypeStruct((g, t, N_HEADS * V_DIM), BF16),
                   jax.ShapeDtypeStruct((g, t, KV_LORA), F32),
                   jax.ShapeDtypeStruct((g, t, QK_ROPE), F32)],
        compiler_params=_params(2),
        name="attn_front",
    )(x, *tables, w_small, qg, kvg, w_qb, w_kb, w_vb)


def _softmax_step(s, v, m, l, acc):
    m_new = jnp.maximum(m, jnp.max(s, axis=-1, keepdims=True))
    alpha = jnp.exp(m - m_new)
    p = jnp.exp(s - m_new)
    l = alpha * l + jnp.sum(p, axis=-1, keepdims=True)
    acc = alpha * acc + _dot(p.astype(BF16), v)
    return m_new, l, acc


def _prompt_attn_kernel(q_ref, k_ref, v_ref, o_ref):
    tq = q_ref.shape[0]
    tk = ATTN_K_TILE
    qi = pl.program_id(2)
    q = q_ref[...]

    def chunk(j, carry, masked):
        start = pl.multiple_of(j * tk, tk)
        s = _dot_nt(q, k_ref[pl.ds(start, tk), :]) * ATTN_SCALE
        if masked:
            row = lax.broadcasted_iota(jnp.int32, (tq, tk), 0)
            col = lax.broadcasted_iota(jnp.int32, (tq, tk), 1)
            s = jnp.where(col <= row, s, MASK_VALUE)
        return _softmax_step(s, v_ref[pl.ds(start, tk), :], *carry)

    init = (jnp.full((tq, 1), MASK_VALUE, F32), jnp.zeros((tq, 1), F32), jnp.zeros((tq, V_DIM), F32))
    carry = lax.fori_loop(0, qi, lambda j, c: chunk(j, c, False), init)
    _, l, acc = chunk(qi, carry, True)
    o_ref[...] = (acc / l).astype(o_ref.dtype)


def _prompt_attention(q, k, v):
    b, t, _ = q.shape
    tq = ATTN_Q_TILE
    assert tq == ATTN_K_TILE and t % tq == 0
    return pl.pallas_call(
        _prompt_attn_kernel,
        grid=(b, N_HEADS, t // tq),
        in_specs=[pl.BlockSpec((None, tq, HEAD_PAD), lambda bi, h, i: (bi, i, h)),
                  pl.BlockSpec((None, t, HEAD_PAD), lambda bi, h, i: (bi, 0, h)),
                  pl.BlockSpec((None, t, V_DIM), lambda bi, h, i: (bi, 0, h))],
        out_specs=pl.BlockSpec((None, tq, V_DIM), lambda bi, h, i: (bi, i, h)),
        out_shape=jax.ShapeDtypeStruct((b, t, N_HEADS * V_DIM), BF16),
        compiler_params=_params(3),
        name="prompt_attention",
    )(q, k, v)


def _sample_attn_kernel(pt_ref, ql_ref, qr_ref, nl_ref, nr_ref, *rest):
    del pt_ref
    n = PAGES_PER_STEP
    lat_pages = rest[0:n]
    rope_pages = rest[n:2 * n]
    o_ref, kl_scr, kr_scr, m_scr, l_scr, acc_scr = rest[2 * n:]
    c = pl.program_id(1)
    rows = ql_ref.shape[0]

    @pl.when(c == 0)
    def _():
        m_scr[...] = jnp.full(m_scr.shape, MASK_VALUE, F32)
        l_scr[...] = jnp.zeros(l_scr.shape, F32)
        acc_scr[...] = jnp.zeros(acc_scr.shape, F32)

    for j in range(n):
        kl_scr[j * PAGE_SIZE:(j + 1) * PAGE_SIZE, :] = lat_pages[j][...].astype(BF16)
        kr_scr[j * PAGE_SIZE:(j + 1) * PAGE_SIZE, :] = rope_pages[j][...].astype(BF16)
    ql = ql_ref[...]
    qr = qr_ref[...]
    kl = kl_scr[...]
    s = (_dot_nt(ql, kl) + _dot_nt(qr, kr_scr[...])) * ATTN_SCALE
    m, l, acc = _softmax_step(s, kl, m_scr[...], l_scr[...], acc_scr[...])
    m_scr[...] = m
    l_scr[...] = l
    acc_scr[...] = acc

    @pl.when(c == pl.num_programs(1) - 1)
    def _():
        nl = nl_ref[...].astype(BF16)
        s_new = (_dot_nt(ql, nl) + _dot_nt(qr, nr_ref[...].astype(BF16))) * ATTN_SCALE
        tok = lax.shift_right_logical(lax.broadcasted_iota(jnp.int32, (rows, NEW_KEY_PAD), 0),
                                      int(np.log2(N_HEADS)))
        key = lax.broadcasted_iota(jnp.int32, (rows, NEW_KEY_PAD), 1)
        s_new = jnp.where(key <= tok, s_new, MASK_VALUE)
        _, l2, acc2 = _softmax_step(s_new, nl, m, l, acc)
        o_ref[...] = (acc2 / l2).astype(o_ref.dtype)


def _sample_attention(page_table, q_lat, q_rope, new_lat, new_rope, cache_lat, cache_rope):
    s, rows, _ = q_lat.shape
    n_pages = page_table.shape[1]
    n = PAGES_PER_STEP
    assert n_pages % n == 0 and rows // N_HEADS <= NEW_KEY_PAD
    seq = lambda r, w: pl.BlockSpec((None, r, w), lambda b, c, pt: (b, 0, 0))

    def page_spec(width, j):
        return pl.BlockSpec((None, PAGE_SIZE, width), lambda b, c, pt: (pt[b, c * n + j], 0, 0))

    grid_spec = pltpu.PrefetchScalarGridSpec(
        num_scalar_prefetch=1,
        grid=(s, n_pages // n),
        in_specs=[seq(rows, KV_LORA), seq(rows, QK_ROPE), seq(NEW_KEY_PAD, KV_LORA), seq(NEW_KEY_PAD, QK_ROPE)]
        + [page_spec(KV_LORA, j) for j in range(n)]
        + [page_spec(QK_ROPE, j) for j in range(n)],
        out_specs=seq(rows, KV_LORA),
        scratch_shapes=[pltpu.VMEM((n * PAGE_SIZE, KV_LORA), BF16),
                        pltpu.VMEM((n * PAGE_SIZE, QK_ROPE), BF16),
                        pltpu.VMEM((rows, 1), F32), pltpu.VMEM((rows, 1), F32),
                        pltpu.VMEM((rows, KV_LORA), F32)],
    )
    return pl.pallas_call(
        _sample_attn_kernel,
        grid_spec=grid_spec,
        out_shape=jax.ShapeDtypeStruct((s, rows, KV_LORA), BF16),
        compiler_params=_params(2),
        name="sample_attention",
    )(page_table, q_lat, q_rope, new_lat, new_rope, *([cache_lat] * n), *([cache_rope] * n))


def _head_matmul_kernel(x_ref, w_ref, o_ref):
    o_ref[...] = _dot(x_ref[...], w_ref[...]).astype(o_ref.dtype)


def _head_matmul(x, w, *, in_block_stride):
    m = x.shape[0]
    h, xw, yw = w.shape
    return pl.pallas_call(
        _head_matmul_kernel,
        grid=(h,),
        in_specs=[pl.BlockSpec((m, xw), lambda i: (0, i * in_block_stride)),
                  pl.BlockSpec((None, xw, yw), lambda i: (i, 0, 0))],
        out_specs=pl.BlockSpec((m, yw), lambda i: (0, i)),
        out_shape=jax.ShapeDtypeStruct((m, h * yw), BF16),
        compiler_params=_params(1),
        name="head_matmul",
    )(x, w)


def _merge_ln_kernel(o_ref, ga_ref, sg_ref, x_ref, w_ao_ref, w_mix_ref, g_ref, b_ref, h1_ref, *, alpha):
    m = _dot(o_ref[...], w_ao_ref[...])
    mix = (ga_ref[...] + sg_ref[...] * m).astype(BF16)
    z = _dot(mix, w_mix_ref[...])
    h1_ref[...] = _layer_norm(alpha * x_ref[...] + z, g_ref[...], b_ref[...])


def _merge_ln(o, ga, sg, x, w_ao, w_mix, g, b, *, alpha):
    t = x.shape[0]
    tm = min(TOKEN_TILE, t)
    tok = lambda w: pl.BlockSpec((tm, w), lambda i: (i, 0))
    return pl.pallas_call(
        functools.partial(_merge_ln_kernel, alpha=alpha),
        grid=(t // tm,),
        in_specs=[tok(o.shape[1]), tok(D_MODEL), tok(D_MODEL), tok(D_MODEL),
                  _const_spec(w_ao.shape), _const_spec(w_mix.shape),
                  _const_spec(g.shape), _const_spec(b.shape)],
        out_specs=tok(D_MODEL),
        out_shape=jax.ShapeDtypeStruct((t, D_MODEL), F32),
        compiler_params=_params(1),
        name="merge_ln",
    )(o, ga, sg, x, w_ao, w_mix, g, b)


FF_CHUNK = 1024


def _ffn_ln_kernel(h1_ref, w1_ref, w2_ref, g_ref, b_ref, y_ref, *, alpha):
    h1 = h1_ref[...]
    hb = h1.astype(BF16)
    f = jnp.zeros(h1.shape, F32)
    for c in range(0, D_FF, FF_CHUNK):
        hid = jnp.square(jax.nn.relu(_dot(hb, w1_ref[:, c:c + FF_CHUNK]))).astype(BF16)
        f = f + _dot(hid, w2_ref[c:c + FF_CHUNK, :])
    y_ref[...] = _layer_norm(alpha * h1 + f, g_ref[...], b_ref[...])


def _ffn_ln(h1, w1, w2, g, b, *, alpha):
    t = h1.shape[0]
    tm = min(TOKEN_TILE, t)
    tok = pl.BlockSpec((tm, D_MODEL), lambda i: (i, 0))
    return pl.pallas_call(
        functools.partial(_ffn_ln_kernel, alpha=alpha),
        grid=(t // tm,),
        in_specs=[tok, _const_spec(w1.shape), _const_spec(w2.shape),
                  _const_spec(g.shape), _const_spec(b.shape)],
        out_specs=tok,
        out_shape=jax.ShapeDtypeStruct((t, D_MODEL), F32),
        compiler_params=_params(1),
        name="ffn_ln",
    )(h1, w1, w2, g, b)


def _rope_tables(pos):
    freqs = ROPE_THETA ** (-jnp.arange(0, QK_ROPE, 2, dtype=F32) / QK_ROPE)
    ang = pos.astype(F32)[:, None] * freqs[None, :]
    c, s = jnp.cos(ang), jnp.sin(ang)
    z = jnp.zeros_like(c)
    return (jnp.concatenate([c, c, z, z], axis=1),
            jnp.concatenate([-s, z, z, z], axis=1),
            jnp.concatenate([z, s, z, z], axis=1))


def _layer_weights(w_in, conv_w, q_norm_g, w_qb, kv_norm_g, w_kb, w_vb, w_conv_out, w_attn_out,
                   w_mix_out, ln1_g, ln1_b, w_ff1, w_ff2, ln2_g, ln2_b):
    c = D_CONV
    o_q = 3 * c
    o_kv = o_q + Q_LORA
    o_kr = o_kv + KV_LORA
    o_gc = o_kr + QK_ROPE
    o_ga = o_gc + D_MODEL
    d = w_in.shape[0]
    w_in_cg = jnp.concatenate([w_in[:, 0:o_q], w_in[:, o_gc:o_ga + D_MODEL]], axis=1).astype(BF16)
    w_small = jnp.concatenate(
        [w_in[:, o_q:o_gc], jnp.zeros((d, LANES - QK_ROPE), w_in.dtype)], axis=1).astype(BF16)
    wq = w_qb.reshape(Q_LORA, N_HEADS, QK_NOPE + QK_ROPE)
    wq = jnp.concatenate(
        [wq, jnp.zeros((Q_LORA, N_HEADS, HEAD_PAD - QK_NOPE - QK_ROPE), w_qb.dtype)], axis=2)
    return dict(
        w_in_cg=w_in_cg, w_small=w_small, conv_w=conv_w,
        qg=q_norm_g[None, :], kvg=kv_norm_g[None, :],
        w_qb=wq.reshape(Q_LORA, N_HEADS * HEAD_PAD).astype(BF16),
        w_kb=w_kb.reshape(KV_LORA, N_HEADS * QK_NOPE).astype(BF16),
        w_vb=w_vb.reshape(KV_LORA, N_HEADS * V_DIM).astype(BF16),
        w_kb_heads=jnp.transpose(w_kb, (1, 2, 0)).astype(BF16),
        w_vb_heads=jnp.transpose(w_vb, (1, 0, 2)).astype(BF16),
        w_co=w_conv_out.astype(BF16), w_ao=w_attn_out.astype(BF16), w_mix=w_mix_out.astype(BF16),
        ln1_g=ln1_g[None, :], ln1_b=ln1_b[None, :],
        w_ff1=w_ff1.astype(BF16), w_ff2=w_ff2.astype(BF16),
        ln2_g=ln2_g[None, :], ln2_b=ln2_b[None, :],
    )


def _tail(o, ga, sg, x, lw, alpha):
    h1 = _merge_ln(o, ga, sg, x, lw["w_ao"], lw["w_mix"], lw["ln1_g"], lw["ln1_b"], alpha=alpha)
    return _ffn_ln(h1, lw["w_ff1"], lw["w_ff2"], lw["ln2_g"], lw["ln2_b"], alpha=alpha)


def _prompt_layer(x, conv_prev, lw, alpha):
    b, t, d = x.shape
    ga, sg, conv_new = _conv_gate(x, conv_prev, lw["w_in_cg"], lw["conv_w"], lw["w_co"], sample_rows=None)
    q, k, v, lat, krope = _attn_front(x, _rope_tables(jnp.arange(t)), lw["w_small"], lw["qg"], lw["kvg"],
                                      lw["w_qb"], lw["w_kb"], lw["w_vb"])
    o = _prompt_attention(q, k, v)
    y = _tail(o.reshape(b * t, -1), ga.reshape(b * t, d), sg.reshape(b * t, d), x.reshape(b * t, d), lw, alpha)
    return y.reshape(b, t, d), lat, krope, conv_new


def _sample_layer(x, conv_prev, cache_lat, cache_rope, page_table, lw, alpha):
    s, t, d = x.shape
    past = page_table.shape[1] * PAGE_SIZE
    xt = jnp.transpose(x, (1, 0, 2)).reshape(1, t * s, d)
    state = jnp.transpose(conv_prev, (1, 0, 2)).reshape(1, (CONV_W - 1) * s, D_CONV)
    ga, sg, conv_new = _conv_gate(xt, state, lw["w_in_cg"], lw["conv_w"], lw["w_co"], sample_rows=s)
    tables = _rope_tables(jnp.repeat(past + jnp.arange(t), s))
    q, _, _, lat, krope = _attn_front(xt, tables, lw["w_small"], lw["qg"], lw["kvg"],
                                      lw["w_qb"], lw["w_kb"], lw["w_vb"])
    q = q[0]
    q_lat = _head_matmul(q, lw["w_kb_heads"], in_block_stride=HEAD_PAD // QK_NOPE)

    def per_seq(a, w):
        return jnp.transpose(a.reshape(t, s, N_HEADS, w), (1, 0, 2, 3)).reshape(s, t * N_HEADS, w)

    q_rope = per_seq(q.reshape(t * s, N_HEADS, HEAD_PAD)[:, :, QK_NOPE:QK_NOPE + QK_ROPE].reshape(t * s, -1), QK_ROPE)
    lat_seq = jnp.transpose(lat.reshape(t, s, KV_LORA), (1, 0, 2))
    krope_seq = jnp.transpose(krope.reshape(t, s, QK_ROPE), (1, 0, 2))
    pad = ((0, 0), (0, NEW_KEY_PAD - t), (0, 0))
    o_lat = _sample_attention(page_table, per_seq(q_lat, KV_LORA), q_rope,
                              jnp.pad(lat_seq, pad), jnp.pad(krope_seq, pad), cache_lat, cache_rope)
    o_lat = jnp.transpose(o_lat.reshape(s, t, N_HEADS, KV_LORA), (1, 0, 2, 3)).reshape(t * s, -1)
    o = _head_matmul(o_lat, lw["w_vb_heads"], in_block_stride=1)
    y = _tail(o, ga[0], sg[0], xt[0], lw, alpha)
    y = jnp.transpose(y.reshape(t, s, d), (1, 0, 2))
    conv_new = jnp.transpose(conv_new.reshape(CONV_W - 1, s, D_CONV), (1, 0, 2))
    return y, lat_seq, krope_seq, conv_new


def kernel(x_prompt, x_sample, cache_latent, cache_krope, state_conv, page_table, w_in, conv_w, q_norm_g, w_qb, kv_norm_g, w_kb, w_vb, w_conv_out, w_attn_out, w_mix_out, ln1_g, ln1_b, w_ff1, w_ff2, ln2_g, ln2_b):
    depth = w_in.shape[0]
    alpha = (2 * depth) ** 0.25
    conv_zero = jnp.zeros((x_prompt.shape[0], CONV_W - 1, D_CONV), x_prompt.dtype)
    y_p, y_s = x_prompt, x_sample
    outs = [[] for _ in range(6)]
    for l in range(depth):
        lw = _layer_weights(w_in[l], conv_w[l], q_norm_g[l], w_qb[l], kv_norm_g[l], w_kb[l], w_vb[l],
                            w_conv_out[l], w_attn_out[l], w_mix_out[l], ln1_g[l], ln1_b[l],
                            w_ff1[l], w_ff2[l], ln2_g[l], ln2_b[l])
        y_p, lp, rp, cp = _prompt_layer(y_p, conv_zero, lw, alpha)
        y_s, ls, rs, cs = _sample_layer(y_s, state_conv[l], cache_latent[l], cache_krope[l], page_table, lw, alpha)
        for acc, val in zip(outs, (lp, rp, cp, ls, rs, cs)):
            acc.append(val)
    return (y_p, y_s) + tuple(jnp.stack(o) for o in outs)
```

```python
import functools

import jax
import jax.numpy as jnp
import numpy as np
from jax import lax
from jax.experimental import pallas as pl
from jax.experimental.pallas import tpu as pltpu

F32 = jnp.float32
BF16 = jnp.bfloat16

D_MODEL = 1024
D_CONV = 1024
CONV_W = 3
N_HEADS = 16
QK_NOPE = 128
QK_ROPE = 64
V_DIM = 128
Q_LORA = 256
KV_LORA = 256
ROPE_THETA = 10000.0
D_FF = 4 * D_MODEL
NORM_EPS = 1e-5
PAGE_SIZE = 128
MASK_VALUE = -1e30
ATTN_SCALE = (QK_NOPE + QK_ROPE) ** -0.5

LANES = 128
HEAD_PAD = 2 * LANES
SMALL_W = Q_LORA + KV_LORA + LANES
VMEM_LIMIT = 56 * 1024 * 1024

QK_LOG2_SCALE = ATTN_SCALE * float(np.log2(np.e))

TOKEN_TILE = 512
ATTN_Q_TILE = 512
ATTN_HEADS_PER_STEP = 1
PAGES_PER_STEP = 32
SAMPLE_CHAINS = 2
NEW_KEY_PAD = 16


def _dot(a, b):
    return jnp.dot(a, b, preferred_element_type=F32)


def _dot_nt(a, b):
    return lax.dot_general(a, b, (((1,), (1,)), ((), ())), preferred_element_type=F32)


def _const_spec(shape):
    zeros = (0,) * len(shape)
    return pl.BlockSpec(shape, lambda *_: zeros, pipeline_mode=pl.Buffered(1))


def _params(n_grid_dims, flags=None):
    return pltpu.CompilerParams(
        dimension_semantics=("arbitrary",) * n_grid_dims,
        vmem_limit_bytes=VMEM_LIMIT,
        flags=flags,
    )


def _rms_norm(x, g):
    ms = jnp.mean(jnp.square(x), axis=-1, keepdims=True)
    return x * lax.rsqrt(ms + NORM_EPS) * g


def _layer_norm(x, g, b):
    mu = jnp.mean(x, axis=-1, keepdims=True)
    xc = x - mu
    var = jnp.mean(jnp.square(xc), axis=-1, keepdims=True)
    return xc * lax.rsqrt(var + NORM_EPS) * g + b


def _rope_padded(r, cos, sin_lo, sin_hi):
    return (r * cos
            + pltpu.roll(r, LANES - QK_ROPE // 2, axis=1) * sin_lo
            + pltpu.roll(r, QK_ROPE // 2, axis=1) * sin_hi)


def _conv_gate_kernel(x_ref, state_ref, w_in_ref, conv_w_ref, w_co_ref,
                      ga_ref, sg_ref, conv_new_ref, u_scr, *, sample_rows):
    tm = x_ref.shape[0]
    xb = x_ref[...].astype(BF16)
    c = D_CONV
    u = _dot(xb, w_in_ref[:, c:2 * c]) * _dot(xb, w_in_ref[:, 2 * c:3 * c])
    w0 = conv_w_ref[0:1, :]
    w1 = conv_w_ref[1:2, :]
    w2 = conv_w_ref[2:3, :]
    if sample_rows is None:
        @pl.when(pl.program_id(1) == 0)
        def _():
            u_scr[0:6, :] = jnp.zeros((6, c), F32)
            u_scr[6:8, :] = state_ref[...]
        u_scr[8:8 + tm, :] = u
        v = w2 * u + w1 * u_scr[7:7 + tm, :] + w0 * u_scr[6:6 + tm, :]
        tail = u_scr[tm:tm + 8, :]
        u_scr[0:8, :] = tail
        conv_new_ref[...] = tail[6:8, :]
    else:
        s = sample_rows
        u_scr[0:2 * s, :] = state_ref[...]
        u_scr[2 * s:2 * s + tm, :] = u
        v = w2 * u + w1 * u_scr[s:s + tm, :] + w0 * u_scr[0:tm, :]
        conv_new_ref[...] = u_scr[tm:tm + 2 * s, :]
    a_in = (_dot(xb, w_in_ref[:, 0:c]) * v).astype(BF16)
    a = _dot(a_in, w_co_ref[...])
    ga_ref[...] = jax.nn.sigmoid(_dot(xb, w_in_ref[:, 3 * c:3 * c + D_MODEL])) * a
    sg_ref[...] = jax.nn.sigmoid(_dot(xb, w_in_ref[:, 3 * c + D_MODEL:3 * c + 2 * D_MODEL]))


def _conv_gate(x, state, w_in_cg, conv_w, w_co, *, sample_rows):
    g, t, d = x.shape
    tm = min(TOKEN_TILE, t)
    n_state = state.shape[1]
    scr_rows = 8 + tm if sample_rows is None else 2 * sample_rows + tm
    tok = lambda w: pl.BlockSpec((None, tm, w), lambda b, i: (b, i, 0))
    return pl.pallas_call(
        functools.partial(_conv_gate_kernel, sample_rows=sample_rows),
        grid=(g, t // tm),
        in_specs=[tok(d),
                  pl.BlockSpec((None, n_state, D_CONV), lambda b, i: (b, 0, 0)),
                  _const_spec(w_in_cg.shape), _const_spec(conv_w.shape), _const_spec(w_co.shape)],
        out_specs=[tok(D_MODEL), tok(D_MODEL),
                   pl.BlockSpec((None, n_state, D_CONV), lambda b, i: (b, 0, 0))],
        out_shape=[jax.ShapeDtypeStruct((g, t, D_MODEL), F32),
                   jax.ShapeDtypeStruct((g, t, D_MODEL), F32),
                   jax.ShapeDtypeStruct((g, n_state, D_CONV), F32)],
        scratch_shapes=[pltpu.VMEM((scr_rows, D_CONV), F32)],
        compiler_params=_params(2),
        name="conv_gate",
    )(x, state, w_in_cg, conv_w, w_co)


def _attn_front_kernel(x_ref, cos_ref, sin_lo_ref, sin_hi_ref, w_small_ref, qg_ref, kvg_ref,
                       w_qb_ref, w_kb_ref, w_vb_ref,
                       q_ref, k_ref, vt_ref, lat_ref, krope_ref):
    xb = x_ref[...].astype(BF16)
    small = _dot(xb, w_small_ref[...])
    cos = cos_ref[...]
    sin_lo = sin_lo_ref[...]
    sin_hi = sin_hi_ref[...]

    qn = _rms_norm(small[:, 0:Q_LORA], qg_ref[...]).astype(BF16)
    q = _dot(qn, w_qb_ref[...]) * QK_LOG2_SCALE
    for h in range(N_HEADS):
        lo = h * HEAD_PAD
        q_ref[:, lo:lo + LANES] = q[:, lo:lo + LANES].astype(BF16)
        q_ref[:, lo + LANES:lo + HEAD_PAD] = _rope_padded(
            q[:, lo + LANES:lo + HEAD_PAD], cos, sin_lo, sin_hi).astype(BF16)

    lat = _rms_norm(small[:, Q_LORA:Q_LORA + KV_LORA], kvg_ref[...])
    lat_ref[...] = lat
    latb = lat.astype(BF16)
    krope = _rope_padded(small[:, Q_LORA + KV_LORA:SMALL_W], cos, sin_lo, sin_hi)
    krope_ref[...] = krope[:, 0:QK_ROPE]
    kropeb = krope.astype(BF16)
    kn = _dot(latb, w_kb_ref[...]).astype(BF16)
    for h in range(N_HEADS):
        lo = h * HEAD_PAD
        k_ref[:, lo:lo + LANES] = kn[:, h * QK_NOPE:(h + 1) * QK_NOPE]
        k_ref[:, lo + LANES:lo + HEAD_PAD] = kropeb
    vt_ref[...] = _dot_nt(w_vb_ref[...], latb).astype(BF16)


def _attn_front(x, tables, w_small, qg, kvg, w_qb, w_kb, w_vb_t):
    g, t, d = x.shape
    tm = min(TOKEN_TILE, t)
    tok = lambda w: pl.BlockSpec((None, tm, w), lambda b, i: (b, i, 0))
    tab = pl.BlockSpec((tm, LANES), lambda b, i: (i, 0))
    return pl.pallas_call(
        _attn_front_kernel,
        grid=(g, t // tm),
        in_specs=[tok(d), tab, tab, tab,
                  _const_spec(w_small.shape), _const_spec(qg.shape), _const_spec(kvg.shape),
                  _const_spec(w_qb.shape), _const_spec(w_kb.shape), _const_spec(w_vb_t.shape)],
        out_specs=[tok(N_HEADS * HEAD_PAD), tok(N_HEADS * HEAD_PAD),
                   pl.BlockSpec((None, None, N_HEADS * V_DIM, tm), lambda b, i: (b, i, 0, 0)),
                   tok(KV_LORA), tok(QK_ROPE)],
        out_shape=[jax.ShapeDtypeStruct((g, t, N_HEADS * HEAD_PAD), BF16),
                   jax.ShapeDtypeStruct((g, t, N_HEADS * HEAD_PAD), BF16),
                   jax.ShapeDtypeStruct((g, t // tm, N_HEADS * V_DIM, tm), BF16),
                   jax.ShapeDtypeStruct((g, t, KV_LORA), F32),
                   jax.ShapeDtypeStruct((g, t, QK_ROPE), F32)],
        compiler_params=_params(2),
        name="attn_front",
    )(x, *tables, w_small, qg, kvg, w_qb, w_kb, w_vb_t)


def _softmax_step(s, v, m, l, acc):
    m_new = jnp.maximum(m, jnp.max(s, axis=-1, keepdims=True))
    alpha = jnp.exp2(m - m_new)
    p = jnp.exp2(s - m_new)
    l = alpha * l + jnp.sum(p, axis=-1, keepdims=True)
    acc = alpha * acc + _dot(p.astype(BF16), v)
    return m_new, l, acc


def _softmax_init(rows, width):
    return (jnp.full((rows, 1), MASK_VALUE, F32), jnp.zeros((rows, 1), F32), jnp.zeros((rows, width), F32))


def _prompt_attn_kernel(q_ref, k_ref, vt_ref, o_ref):
    tq = q_ref.shape[0]
    tk = vt_ref.shape[2]
    qi = pl.program_id(2)
    heads = range(q_ref.shape[1] // HEAD_PAD)
    qs = [q_ref[:, h * HEAD_PAD:(h + 1) * HEAD_PAD] for h in heads]

    def scores(j, h):
        start = pl.multiple_of(j * tk, tk)
        return _dot_nt(k_ref[pl.ds(start, tk), h * HEAD_PAD:(h + 1) * HEAD_PAD], qs[h])

    def update(j, h, s_t, m, l, acc_t):
        m_new = jnp.maximum(m, jnp.max(s_t, axis=0, keepdims=True))
        alpha = jnp.exp2(m - m_new)
        p_t = jnp.exp2(s_t - m_new)
        l = alpha * l + jnp.sum(p_t, axis=0, keepdims=True)
        acc_t = alpha * acc_t + _dot(vt_ref[j, h * V_DIM:(h + 1) * V_DIM, :], p_t.astype(BF16))
        return m_new, l, acc_t

    def body(j, carry):
        nxt = [scores(j + 1, h) for h in heads]
        return tuple((nxt[h],) + update(j, h, *carry[h]) for h in heads)

    init = tuple((scores(0, h), jnp.full((1, tq), MASK_VALUE, F32), jnp.zeros((1, tq), F32),
                  jnp.zeros((V_DIM, tq), F32)) for h in heads)
    carry = lax.fori_loop(0, qi, body, init)
    key = lax.broadcasted_iota(jnp.int32, (tk, tq), 0)
    qry = lax.broadcasted_iota(jnp.int32, (tk, tq), 1)
    for h in heads:
        s_t, m, l, acc_t = carry[h]
        _, l, acc_t = update(qi, h, jnp.where(key <= qry, s_t, MASK_VALUE), m, l, acc_t)
        o_ref[:, h * V_DIM:(h + 1) * V_DIM] = jnp.transpose(acc_t / l).astype(o_ref.dtype)


def _prompt_attention(q, k, v_t):
    b, t, _ = q.shape
    tq = ATTN_Q_TILE
    n_k, _, tk = v_t.shape[1:]
    g = ATTN_HEADS_PER_STEP
    assert tq == tk and t == n_k * tk and N_HEADS % g == 0
    return pl.pallas_call(
        _prompt_attn_kernel,
        grid=(b, N_HEADS // g, t // tq),
        in_specs=[pl.BlockSpec((None, tq, g * HEAD_PAD), lambda bi, h, i: (bi, i, h)),
                  pl.BlockSpec((None, t, g * HEAD_PAD), lambda bi, h, i: (bi, 0, h)),
                  pl.BlockSpec((None, n_k, g * V_DIM, tk), lambda bi, h, i: (bi, 0, h, 0))],
        out_specs=pl.BlockSpec((None, tq, g * V_DIM), lambda bi, h, i: (bi, i, h)),
        out_shape=jax.ShapeDtypeStruct((b, t, N_HEADS * V_DIM), BF16),
        compiler_params=_params(3),
        name="prompt_attention",
    )(q, k, v_t)


def _sample_attn_kernel(pt_ref, ql_ref, qr_ref, nl_ref, nr_ref, lat_hbm, ropet_hbm,
                        o_ref, lat_buf, rope_buf, sems, m_scr, l_scr, acc_scr):
    n = PAGES_PER_STEP
    b = pl.program_id(0)
    c = pl.program_id(1)
    n_chunks = pl.num_programs(1)
    step = b * n_chunks + c
    n_steps = pl.num_programs(0) * n_chunks
    slot = lax.rem(step, 2)
    rows = ql_ref.shape[0]

    def page_copies(bb, cc, sl):
        copies = []
        for j in range(n):
            page = pt_ref[bb, cc * n + j]
            keys = pl.ds(j * PAGE_SIZE, PAGE_SIZE)
            copies.append(pltpu.make_async_copy(lat_hbm.at[page], lat_buf.at[sl, keys, :], sems.at[0, sl]))
            copies.append(pltpu.make_async_copy(ropet_hbm.at[page], rope_buf.at[sl, :, keys], sems.at[1, sl]))
        return copies

    @pl.when(step == 0)
    def _():
        for cp in page_copies(b, c, slot):
            cp.start()

    @pl.when(step + 1 < n_steps)
    def _():
        wrap = c + 1 == n_chunks
        for cp in page_copies(jnp.where(wrap, b + 1, b), jnp.where(wrap, 0, c + 1), 1 - slot):
            cp.start()

    @pl.when(c == 0)
    def _():
        m_scr[...] = jnp.full(m_scr.shape, MASK_VALUE, F32)
        l_scr[...] = jnp.zeros(l_scr.shape, F32)
        acc_scr[...] = jnp.zeros(acc_scr.shape, F32)

    for cp in page_copies(b, c, slot):
        cp.wait()

    ql = ql_ref[...]
    qr = qr_ref[...]
    span = n * PAGE_SIZE // SAMPLE_CHAINS
    for h in range(SAMPLE_CHAINS):
        kl = lat_buf[slot, h * span:(h + 1) * span, :].astype(BF16)
        kr_t = rope_buf[slot, :, h * span:(h + 1) * span].astype(BF16)
        s = _dot_nt(ql, kl) + _dot(qr, kr_t)
        m, l, acc = _softmax_step(s, kl, m_scr[h], l_scr[h], acc_scr[h])
        m_scr[h] = m
        l_scr[h] = l
        acc_scr[h] = acc

    @pl.when(c == n_chunks - 1)
    def _():
        m, l, acc = m_scr[0], l_scr[0], acc_scr[0]
        for h in range(1, SAMPLE_CHAINS):
            m_new = jnp.maximum(m, m_scr[h])
            wa = jnp.exp2(m - m_new)
            wb = jnp.exp2(m_scr[h] - m_new)
            l = wa * l + wb * l_scr[h]
            acc = wa * acc + wb * acc_scr[h]
            m = m_new
        nl = nl_ref[...].astype(BF16)
        s_new = _dot_nt(ql, nl) + _dot_nt(qr, nr_ref[...].astype(BF16))
        tok = lax.shift_right_logical(lax.broadcasted_iota(jnp.int32, (rows, NEW_KEY_PAD), 0),
                                      int(np.log2(N_HEADS)))
        key = lax.broadcasted_iota(jnp.int32, (rows, NEW_KEY_PAD), 1)
        s_new = jnp.where(key <= tok, s_new, MASK_VALUE)
        _, l2, acc2 = _softmax_step(s_new, nl, m, l, acc)
        o_ref[...] = (acc2 / l2).astype(o_ref.dtype)


def _sample_attention(page_table, q_lat, q_rope, new_lat, new_rope, cache_lat, cache_rope_t):
    s, rows, _ = q_lat.shape
    n_pages = page_table.shape[1]
    n = PAGES_PER_STEP
    assert n_pages % n == 0 and rows // N_HEADS <= NEW_KEY_PAD
    seq = lambda r, w: pl.BlockSpec((None, r, w), lambda b, c, pt: (b, 0, 0))
    hbm = pl.BlockSpec(memory_space=pl.ANY)
    grid_spec = pltpu.PrefetchScalarGridSpec(
        num_scalar_prefetch=1,
        grid=(s, n_pages // n),
        in_specs=[seq(rows, KV_LORA), seq(rows, QK_ROPE), seq(NEW_KEY_PAD, KV_LORA), seq(NEW_KEY_PAD, QK_ROPE),
                  hbm, hbm],
        out_specs=seq(rows, KV_LORA),
        scratch_shapes=[pltpu.VMEM((2, n * PAGE_SIZE, KV_LORA), F32),
                        pltpu.VMEM((2, QK_ROPE, n * PAGE_SIZE), F32),
                        pltpu.SemaphoreType.DMA((2, 2)),
                        pltpu.VMEM((SAMPLE_CHAINS, rows, 1), F32),
                        pltpu.VMEM((SAMPLE_CHAINS, rows, 1), F32),
                        pltpu.VMEM((SAMPLE_CHAINS, rows, KV_LORA), F32)],
    )
    return pl.pallas_call(
        _sample_attn_kernel,
        grid_spec=grid_spec,
        out_shape=jax.ShapeDtypeStruct((s, rows, KV_LORA), BF16),
        compiler_params=_params(2),
        name="sample_attention",
    )(page_table, q_lat, q_rope, new_lat, new_rope, cache_lat, cache_rope_t)


def _head_matmul_kernel(x_ref, w_ref, o_ref):
    o_ref[...] = _dot(x_ref[...], w_ref[...]).astype(o_ref.dtype)


def _head_matmul(x, w, *, in_block_stride):
    m = x.shape[0]
    h, xw, yw = w.shape
    return pl.pallas_call(
        _head_matmul_kernel,
        grid=(h,),
        in_specs=[pl.BlockSpec((m, xw), lambda i: (0, i * in_block_stride)),
                  pl.BlockSpec((None, xw, yw), lambda i: (i, 0, 0))],
        out_specs=pl.BlockSpec((m, yw), lambda i: (0, i)),
        out_shape=jax.ShapeDtypeStruct((m, h * yw), BF16),
        compiler_params=_params(1),
        name="head_matmul",
    )(x, w)


def _merge_ln_kernel(o_ref, ga_ref, sg_ref, x_ref, w_ao_ref, w_mix_ref, g_ref, b_ref, h1_ref, *, alpha):
    m = _dot(o_ref[...], w_ao_ref[...])
    mix = (ga_ref[...] + sg_ref[...] * m).astype(BF16)
    z = _dot(mix, w_mix_ref[...])
    h1_ref[...] = _layer_norm(alpha * x_ref[...] + z, g_ref[...], b_ref[...])


def _merge_ln(o, ga, sg, x, w_ao, w_mix, g, b, *, alpha):
    t = x.shape[0]
    tm = min(TOKEN_TILE, t)
    tok = lambda w: pl.BlockSpec((tm, w), lambda i: (i, 0))
    return pl.pallas_call(
        functools.partial(_merge_ln_kernel, alpha=alpha),
        grid=(t // tm,),
        in_specs=[tok(o.shape[1]), tok(D_MODEL), tok(D_MODEL), tok(D_MODEL),
                  _const_spec(w_ao.shape), _const_spec(w_mix.shape),
                  _const_spec(g.shape), _const_spec(b.shape)],
        out_specs=tok(D_MODEL),
        out_shape=jax.ShapeDtypeStruct((t, D_MODEL), F32),
        compiler_params=_params(1),
        name="merge_ln",
    )(o, ga, sg, x, w_ao, w_mix, g, b)


FF_CHUNK = 1024


def _ffn_ln_kernel(h1_ref, w1_ref, w2_ref, g_ref, b_ref, y_ref, *, alpha):
    h1 = h1_ref[...]
    hb = h1.astype(BF16)
    f = jnp.zeros(h1.shape, F32)
    for c in range(0, D_FF, FF_CHUNK):
        hid = jnp.square(jax.nn.relu(_dot(hb, w1_ref[:, c:c + FF_CHUNK]))).astype(BF16)
        f = f + _dot(hid, w2_ref[c:c + FF_CHUNK, :])
    y_ref[...] = _layer_norm(alpha * h1 + f, g_ref[...], b_ref[...])


def _ffn_ln(h1, w1, w2, g, b, *, alpha):
    t = h1.shape[0]
    tm = min(TOKEN_TILE, t)
    tok = pl.BlockSpec((tm, D_MODEL), lambda i: (i, 0))
    return pl.pallas_call(
        functools.partial(_ffn_ln_kernel, alpha=alpha),
        grid=(t // tm,),
        in_specs=[tok, _const_spec(w1.shape), _const_spec(w2.shape),
                  _const_spec(g.shape), _const_spec(b.shape)],
        out_specs=tok,
        out_shape=jax.ShapeDtypeStruct((t, D_MODEL), F32),
        compiler_params=_params(1),
        name="ffn_ln",
    )(h1, w1, w2, g, b)


def _rope_tables(pos):
    freqs = ROPE_THETA ** (-jnp.arange(0, QK_ROPE, 2, dtype=F32) / QK_ROPE)
    ang = pos.astype(F32)[:, None] * freqs[None, :]
    c, s = jnp.cos(ang), jnp.sin(ang)
    z = jnp.zeros_like(c)
    return (jnp.concatenate([c, c, z, z], axis=1),
            jnp.concatenate([-s, z, z, z], axis=1),
            jnp.concatenate([z, s, z, z], axis=1))


def _layer_weights(w_in, conv_w, q_norm_g, w_qb, kv_norm_g, w_kb, w_vb, w_conv_out, w_attn_out,
                   w_mix_out, ln1_g, ln1_b, w_ff1, w_ff2, ln2_g, ln2_b):
    c = D_CONV
    o_q = 3 * c
    o_kv = o_q + Q_LORA
    o_kr = o_kv + KV_LORA
    o_gc = o_kr + QK_ROPE
    o_ga = o_gc + D_MODEL
    d = w_in.shape[0]
    w_in_cg = jnp.concatenate([w_in[:, 0:o_q], w_in[:, o_gc:o_ga + D_MODEL]], axis=1).astype(BF16)
    w_small = jnp.concatenate(
        [w_in[:, o_q:o_gc], jnp.zeros((d, LANES - QK_ROPE), w_in.dtype)], axis=1).astype(BF16)
    wq = w_qb.reshape(Q_LORA, N_HEADS, QK_NOPE + QK_ROPE)
    wq = jnp.concatenate(
        [wq, jnp.zeros((Q_LORA, N_HEADS, HEAD_PAD - QK_NOPE - QK_ROPE), w_qb.dtype)], axis=2)
    return dict(
        w_in_cg=w_in_cg, w_small=w_small, conv_w=conv_w,
        qg=q_norm_g[None, :], kvg=kv_norm_g[None, :],
        w_qb=wq.reshape(Q_LORA, N_HEADS * HEAD_PAD).astype(BF16),
        w_kb=w_kb.reshape(KV_LORA, N_HEADS * QK_NOPE).astype(BF16),
        w_vb_t=w_vb.reshape(KV_LORA, N_HEADS * V_DIM).T.astype(BF16),
        w_kb_heads=jnp.transpose(w_kb, (1, 2, 0)).astype(BF16),
        w_vb_heads=jnp.transpose(w_vb, (1, 0, 2)).astype(BF16),
        w_co=w_conv_out.astype(BF16), w_ao=w_attn_out.astype(BF16), w_mix=w_mix_out.astype(BF16),
        ln1_g=ln1_g[None, :], ln1_b=ln1_b[None, :],
        w_ff1=w_ff1.astype(BF16), w_ff2=w_ff2.astype(BF16),
        ln2_g=ln2_g[None, :], ln2_b=ln2_b[None, :],
    )


def _tail(o, ga, sg, x, lw, alpha):
    h1 = _merge_ln(o, ga, sg, x, lw["w_ao"], lw["w_mix"], lw["ln1_g"], lw["ln1_b"], alpha=alpha)
    return _ffn_ln(h1, lw["w_ff1"], lw["w_ff2"], lw["ln2_g"], lw["ln2_b"], alpha=alpha)


def _prompt_layer(x, conv_prev, lw, alpha):
    b, t, d = x.shape
    ga, sg, conv_new = _conv_gate(x, conv_prev, lw["w_in_cg"], lw["conv_w"], lw["w_co"], sample_rows=None)
    q, k, v_t, lat, krope = _attn_front(x, _rope_tables(jnp.arange(t)), lw["w_small"], lw["qg"], lw["kvg"],
                                        lw["w_qb"], lw["w_kb"], lw["w_vb_t"])
    o = _prompt_attention(q, k, v_t)
    y = _tail(o.reshape(b * t, -1), ga.reshape(b * t, d), sg.reshape(b * t, d), x.reshape(b * t, d), lw, alpha)
    return y.reshape(b, t, d), lat, krope, conv_new


def _sample_layer(x, conv_prev, cache_lat, cache_rope, page_table, lw, alpha):
    s, t, d = x.shape
    past = page_table.shape[1] * PAGE_SIZE
    xt = jnp.transpose(x, (1, 0, 2)).reshape(1, t * s, d)
    state = jnp.transpose(conv_prev, (1, 0, 2)).reshape(1, (CONV_W - 1) * s, D_CONV)
    ga, sg, conv_new = _conv_gate(xt, state, lw["w_in_cg"], lw["conv_w"], lw["w_co"], sample_rows=s)
    tables = _rope_tables(jnp.repeat(past + jnp.arange(t), s))
    q, _, _, lat, krope = _attn_front(xt, tables, lw["w_small"], lw["qg"], lw["kvg"],
                                      lw["w_qb"], lw["w_kb"], lw["w_vb_t"])
    q = q[0]
    q_lat = _head_matmul(q, lw["w_kb_heads"], in_block_stride=HEAD_PAD // QK_NOPE)

    def per_seq(a, w):
        return jnp.transpose(a.reshape(t, s, N_HEADS, w), (1, 0, 2, 3)).reshape(s, t * N_HEADS, w)

    q_rope = per_seq(q.reshape(t * s, N_HEADS, HEAD_PAD)[:, :, QK_NOPE:QK_NOPE + QK_ROPE].reshape(t * s, -1), QK_ROPE)
    lat_seq = jnp.transpose(lat.reshape(t, s, KV_LORA), (1, 0, 2))
    krope_seq = jnp.transpose(krope.reshape(t, s, QK_ROPE), (1, 0, 2))
    pad = ((0, 0), (0, NEW_KEY_PAD - t), (0, 0))
    o_lat = _sample_attention(page_table, per_seq(q_lat, KV_LORA), q_rope,
                              jnp.pad(lat_seq, pad), jnp.pad(krope_seq, pad),
                              cache_lat, jnp.swapaxes(cache_rope, 1, 2))
    o_lat = jnp.transpose(o_lat.reshape(s, t, N_HEADS, KV_LORA), (1, 0, 2, 3)).reshape(t * s, -1)
    o = _head_matmul(o_lat, lw["w_vb_heads"], in_block_stride=1)
    y = _tail(o, ga[0], sg[0], xt[0], lw, alpha)
    y = jnp.transpose(y.reshape(t, s, d), (1, 0, 2))
    conv_new = jnp.transpose(conv_new.reshape(CONV_W - 1, s, D_CONV), (1, 0, 2))
    return y, lat_seq, krope_seq, conv_new


def kernel(x_prompt, x_sample, cache_latent, cache_krope, state_conv, page_table, w_in, conv_w, q_norm_g, w_qb, kv_norm_g, w_kb, w_vb, w_conv_out, w_attn_out, w_mix_out, ln1_g, ln1_b, w_ff1, w_ff2, ln2_g, ln2_b):
    depth = w_in.shape[0]
    alpha = (2 * depth) ** 0.25
    conv_zero = jnp.zeros((x_prompt.shape[0], CONV_W - 1, D_CONV), x_prompt.dtype)
    y_p, y_s = x_prompt, x_sample
    outs = [[] for _ in range(6)]
    for l in range(depth):
        lw = _layer_weights(w_in[l], conv_w[l], q_norm_g[l], w_qb[l], kv_norm_g[l], w_kb[l], w_vb[l],
                            w_conv_out[l], w_attn_out[l], w_mix_out[l], ln1_g[l], ln1_b[l],
                            w_ff1[l], w_ff2[l], ln2_g[l], ln2_b[l])
        y_p, lp, rp, cp = _prompt_layer(y_p, conv_zero, lw, alpha)
        y_s, ls, rs, cs = _sample_layer(y_s, state_conv[l], cache_latent[l], cache_krope[l], page_table, lw, alpha)
        for acc, val in zip(outs, (lp, rp, cp, ls, rs, cs)):
            acc.append(val)
    return (y_p, y_s) + tuple(jnp.stack(o) for o in outs)
```

```python
import functools

import jax
import jax.numpy as jnp
import numpy as np
from jax import lax
from jax.experimental import pallas as pl
from jax.experimental.pallas import tpu as pltpu

F32 = jnp.float32
BF16 = jnp.bfloat16

D_MODEL = 1024
D_CONV = 1024
CONV_W = 3
N_HEADS = 16
QK_NOPE = 128
QK_ROPE = 64
V_DIM = 128
Q_LORA = 256
KV_LORA = 256
ROPE_THETA = 10000.0
D_FF = 4 * D_MODEL
NORM_EPS = 1e-5
PAGE_SIZE = 128
MASK_VALUE = -1e30
ATTN_SCALE = (QK_NOPE + QK_ROPE) ** -0.5

LANES = 128
HEAD_PAD = 2 * LANES
SMALL_W = Q_LORA + KV_LORA + LANES
VMEM_LIMIT = 56 * 1024 * 1024

QK_LOG2_SCALE = ATTN_SCALE * float(np.log2(np.e))

TOKEN_TILE = 512
ATTN_Q_TILE = 512
ATTN_HEADS_PER_STEP = 4
REDUCE_FANIN = 8
PAGES_PER_STEP = 32
SAMPLE_CHAINS = 4
NEW_KEY_PAD = 16


def _dot(a, b):
    return jnp.dot(a, b, preferred_element_type=F32)


def _dot_nt(a, b):
    return lax.dot_general(a, b, (((1,), (1,)), ((), ())), preferred_element_type=F32)


def _const_spec(shape):
    zeros = (0,) * len(shape)
    return pl.BlockSpec(shape, lambda *_: zeros, pipeline_mode=pl.Buffered(1))


def _params(n_grid_dims, flags=None):
    return pltpu.CompilerParams(
        dimension_semantics=("arbitrary",) * n_grid_dims,
        vmem_limit_bytes=VMEM_LIMIT,
        flags=flags,
    )


def _rms_norm(x, g):
    ms = jnp.mean(jnp.square(x), axis=-1, keepdims=True)
    return x * lax.rsqrt(ms + NORM_EPS) * g


def _layer_norm(x, g, b):
    mu = jnp.mean(x, axis=-1, keepdims=True)
    xc = x - mu
    var = jnp.mean(jnp.square(xc), axis=-1, keepdims=True)
    return xc * lax.rsqrt(var + NORM_EPS) * g + b


def _rope_padded(r, cos, sin_lo, sin_hi):
    return (r * cos
            + pltpu.roll(r, LANES - QK_ROPE // 2, axis=1) * sin_lo
            + pltpu.roll(r, QK_ROPE // 2, axis=1) * sin_hi)


def _conv_gate_kernel(x_ref, state_ref, w_in_ref, conv_w_ref, w_co_ref,
                      ga_ref, sg_ref, conv_new_ref, u_scr, *, sample_rows):
    tm = x_ref.shape[0]
    xb = x_ref[...].astype(BF16)
    c = D_CONV
    u = _dot(xb, w_in_ref[:, c:2 * c]) * _dot(xb, w_in_ref[:, 2 * c:3 * c])
    w0 = conv_w_ref[0:1, :]
    w1 = conv_w_ref[1:2, :]
    w2 = conv_w_ref[2:3, :]
    if sample_rows is None:
        @pl.when(pl.program_id(1) == 0)
        def _():
            u_scr[0:6, :] = jnp.zeros((6, c), F32)
            u_scr[6:8, :] = state_ref[...]
        u_scr[8:8 + tm, :] = u
        v = w2 * u + w1 * u_scr[7:7 + tm, :] + w0 * u_scr[6:6 + tm, :]
        tail = u_scr[tm:tm + 8, :]
        u_scr[0:8, :] = tail
        conv_new_ref[...] = tail[6:8, :]
    else:
        s = sample_rows
        u_scr[0:2 * s, :] = state_ref[...]
        u_scr[2 * s:2 * s + tm, :] = u
        v = w2 * u + w1 * u_scr[s:s + tm, :] + w0 * u_scr[0:tm, :]
        conv_new_ref[...] = u_scr[tm:tm + 2 * s, :]
    a_in = (_dot(xb, w_in_ref[:, 0:c]) * v).astype(BF16)
    a = _dot(a_in, w_co_ref[...])
    ga_ref[...] = jax.nn.sigmoid(_dot(xb, w_in_ref[:, 3 * c:3 * c + D_MODEL])) * a
    sg_ref[...] = jax.nn.sigmoid(_dot(xb, w_in_ref[:, 3 * c + D_MODEL:3 * c + 2 * D_MODEL]))


def _conv_gate(x, state, w_in_cg, conv_w, w_co, *, sample_rows):
    g, t, d = x.shape
    tm = min(TOKEN_TILE, t)
    n_state = state.shape[1]
    scr_rows = 8 + tm if sample_rows is None else 2 * sample_rows + tm
    tok = lambda w: pl.BlockSpec((None, tm, w), lambda b, i: (b, i, 0))
    return pl.pallas_call(
        functools.partial(_conv_gate_kernel, sample_rows=sample_rows),
        grid=(g, t // tm),
        in_specs=[tok(d),
                  pl.BlockSpec((None, n_state, D_CONV), lambda b, i: (b, 0, 0)),
                  _const_spec(w_in_cg.shape), _const_spec(conv_w.shape), _const_spec(w_co.shape)],
        out_specs=[tok(D_MODEL), tok(D_MODEL),
                   pl.BlockSpec((None, n_state, D_CONV), lambda b, i: (b, 0, 0))],
        out_shape=[jax.ShapeDtypeStruct((g, t, D_MODEL), F32),
                   jax.ShapeDtypeStruct((g, t, D_MODEL), F32),
                   jax.ShapeDtypeStruct((g, n_state, D_CONV), F32)],
        scratch_shapes=[pltpu.VMEM((scr_rows, D_CONV), F32)],
        compiler_params=_params(2),
        name="conv_gate",
    )(x, state, w_in_cg, conv_w, w_co)


def _attn_front_kernel(x_ref, cos_ref, sin_lo_ref, sin_hi_ref, w_small_ref, qg_ref, kvg_ref,
                       w_qb_ref, w_kb_ref, w_vb_ref,
                       q_ref, k_ref, vt_ref, lat_ref, krope_ref):
    xb = x_ref[...].astype(BF16)
    small = _dot(xb, w_small_ref[...])
    cos = cos_ref[...]
    sin_lo = sin_lo_ref[...]
    sin_hi = sin_hi_ref[...]

    qn = _rms_norm(small[:, 0:Q_LORA], qg_ref[...]).astype(BF16)
    q = _dot(qn, w_qb_ref[...]) * QK_LOG2_SCALE
    for h in range(N_HEADS):
        lo = h * HEAD_PAD
        q_ref[:, lo:lo + LANES] = q[:, lo:lo + LANES].astype(BF16)
        q_ref[:, lo + LANES:lo + HEAD_PAD] = _rope_padded(
            q[:, lo + LANES:lo + HEAD_PAD], cos, sin_lo, sin_hi).astype(BF16)

    lat = _rms_norm(small[:, Q_LORA:Q_LORA + KV_LORA], kvg_ref[...])
    lat_ref[...] = lat
    latb = lat.astype(BF16)
    krope = _rope_padded(small[:, Q_LORA + KV_LORA:SMALL_W], cos, sin_lo, sin_hi)
    krope_ref[...] = krope[:, 0:QK_ROPE]
    kropeb = krope.astype(BF16)
    kn = _dot(latb, w_kb_ref[...]).astype(BF16)
    for h in range(N_HEADS):
        lo = h * HEAD_PAD
        k_ref[:, lo:lo + LANES] = kn[:, h * QK_NOPE:(h + 1) * QK_NOPE]
        k_ref[:, lo + LANES:lo + HEAD_PAD] = kropeb
    vt_ref[...] = _dot_nt(w_vb_ref[...], latb).astype(BF16)


def _attn_front(x, tables, w_small, qg, kvg, w_qb, w_kb, w_vb_t):
    g, t, d = x.shape
    tm = min(TOKEN_TILE, t)
    tok = lambda w: pl.BlockSpec((None, tm, w), lambda b, i: (b, i, 0))
    tab = pl.BlockSpec((tm, LANES), lambda b, i: (i, 0))
    return pl.pallas_call(
        _attn_front_kernel,
        grid=(g, t // tm),
        in_specs=[tok(d), tab, tab, tab,
                  _const_spec(w_small.shape), _const_spec(qg.shape), _const_spec(kvg.shape),
                  _const_spec(w_qb.shape), _const_spec(w_kb.shape), _const_spec(w_vb_t.shape)],
        out_specs=[tok(N_HEADS * HEAD_PAD), tok(N_HEADS * HEAD_PAD),
                   pl.BlockSpec((None, None, N_HEADS * V_DIM, tm), lambda b, i: (b, i, 0, 0)),
                   tok(KV_LORA), tok(QK_ROPE)],
        out_shape=[jax.ShapeDtypeStruct((g, t, N_HEADS * HEAD_PAD), BF16),
                   jax.ShapeDtypeStruct((g, t, N_HEADS * HEAD_PAD), BF16),
                   jax.ShapeDtypeStruct((g, t // tm, N_HEADS * V_DIM, tm), BF16),
                   jax.ShapeDtypeStruct((g, t, KV_LORA), F32),
                   jax.ShapeDtypeStruct((g, t, QK_ROPE), F32)],
        compiler_params=_params(2),
        name="attn_front",
    )(x, *tables, w_small, qg, kvg, w_qb, w_kb, w_vb_t)


def _softmax_step(s, v, m, l, acc):
    m_new = jnp.maximum(m, jnp.max(s, axis=-1, keepdims=True))
    alpha = jnp.exp2(m - m_new)
    p = jnp.exp2(s - m_new)
    l = alpha * l + jnp.sum(p, axis=-1, keepdims=True)
    acc = alpha * acc + _dot(p.astype(BF16), v)
    return m_new, l, acc


def _softmax_init(rows, width):
    return (jnp.full((rows, 1), MASK_VALUE, F32), jnp.zeros((rows, 1), F32), jnp.zeros((rows, width), F32))


def _prompt_attn_kernel(q_ref, k_ref, vt_ref, o_ref):
    tq = q_ref.shape[0]
    tk = vt_ref.shape[2]
    qi = pl.program_id(2)
    heads = range(q_ref.shape[1] // HEAD_PAD)
    qs = [q_ref[:, h * HEAD_PAD:(h + 1) * HEAD_PAD] for h in heads]
    n_full = (qi * tq) // tk

    def reduce_keys(x, op):
        part = op(x.reshape(REDUCE_FANIN, tk // REDUCE_FANIN, tq), axis=0)
        return op(part, axis=0, keepdims=True)

    def chunk(j, carry, diagonal=False):
        start = pl.multiple_of(j * tk, tk)

        def scores(h):
            return _dot_nt(k_ref[pl.ds(start, tk), h * HEAD_PAD:(h + 1) * HEAD_PAD], qs[h])

        def softmax(h, s_t):
            m, l, _ = carry[h]
            if diagonal:
                key = j * tk + lax.broadcasted_iota(jnp.int32, (tk, tq), 0)
                qry = qi * tq + lax.broadcasted_iota(jnp.int32, (tk, tq), 1)
                s_t = jnp.where(key <= qry, s_t, MASK_VALUE)
            m_new = jnp.maximum(m, reduce_keys(s_t, jnp.max))
            alpha = jnp.exp2(m - m_new)
            p_t = jnp.exp2(s_t - m_new)
            return m_new, alpha * l + reduce_keys(p_t, jnp.sum), alpha, p_t.astype(BF16)

        def values(h, alpha, p_t):
            return alpha * carry[h][2] + _dot(vt_ref[j, h * V_DIM:(h + 1) * V_DIM, :], p_t)

        s = [scores(h) for h in heads]
        out = []
        for h in heads:
            m_new, l, alpha, p_t = softmax(h, s[h])
            out.append((m_new, l, values(h, alpha, p_t)))
        return tuple(out)

    init = tuple((jnp.full((1, tq), MASK_VALUE, F32), jnp.zeros((1, tq), F32), jnp.zeros((V_DIM, tq), F32))
                 for _ in heads)
    carry = lax.fori_loop(0, n_full, chunk, init)
    for h, (_, l, acc_t) in enumerate(chunk(n_full, carry, diagonal=True)):
        o_ref[:, h * V_DIM:(h + 1) * V_DIM] = jnp.transpose(acc_t / l).astype(o_ref.dtype)


def _prompt_attention(q, k, v_t):
    b, t, _ = q.shape
    tq = ATTN_Q_TILE
    g = ATTN_HEADS_PER_STEP
    n_k, _, tk = v_t.shape[1:]
    assert tk % tq == 0 and t == n_k * tk and N_HEADS % g == 0
    return pl.pallas_call(
        _prompt_attn_kernel,
        grid=(b, N_HEADS // g, t // tq),
        in_specs=[pl.BlockSpec((None, tq, g * HEAD_PAD), lambda bi, h, i: (bi, i, h)),
                  pl.BlockSpec((None, t, g * HEAD_PAD), lambda bi, h, i: (bi, 0, h)),
                  pl.BlockSpec((None, n_k, g * V_DIM, tk), lambda bi, h, i: (bi, 0, h, 0))],
        out_specs=pl.BlockSpec((None, tq, g * V_DIM), lambda bi, h, i: (bi, i, h)),
        out_shape=jax.ShapeDtypeStruct((b, t, N_HEADS * V_DIM), BF16),
        compiler_params=_params(3),
        name="prompt_attention",
    )(q, k, v_t)


def _sample_attn_kernel(pt_ref, ql_ref, qr_ref, nl_ref, nr_ref, lat_hbm, ropet_hbm,
                        o_ref, lat_buf, rope_buf, sems, m_scr, l_scr, acc_scr):
    n = PAGES_PER_STEP
    b = pl.program_id(0)
    c = pl.program_id(1)
    n_chunks = pl.num_programs(1)
    step = b * n_chunks + c
    n_steps = pl.num_programs(0) * n_chunks
    slot = lax.rem(step, 2)
    rows = ql_ref.shape[0]

    def page_copies(bb, cc, sl):
        copies = []
        for j in range(n):
            page = pt_ref[bb, cc * n + j]
            keys = pl.ds(j * PAGE_SIZE, PAGE_SIZE)
            copies.append(pltpu.make_async_copy(lat_hbm.at[page], lat_buf.at[sl, keys, :], sems.at[0, sl]))
            copies.append(pltpu.make_async_copy(ropet_hbm.at[page], rope_buf.at[sl, :, keys], sems.at[1, sl]))
        return copies

    @pl.when(step == 0)
    def _():
        for cp in page_copies(b, c, slot):
            cp.start()

    @pl.when(step + 1 < n_steps)
    def _():
        wrap = c + 1 == n_chunks
        for cp in page_copies(jnp.where(wrap, b + 1, b), jnp.where(wrap, 0, c + 1), 1 - slot):
            cp.start()

    @pl.when(c == 0)
    def _():
        m_scr[...] = jnp.full(m_scr.shape, MASK_VALUE, F32)
        l_scr[...] = jnp.zeros(l_scr.shape, F32)
        acc_scr[...] = jnp.zeros(acc_scr.shape, F32)

    for cp in page_copies(b, c, slot):
        cp.wait()

    ql = ql_ref[...]
    qr = qr_ref[...]
    span = n * PAGE_SIZE // SAMPLE_CHAINS
    kls = [lat_buf[slot, h * span:(h + 1) * span, :].astype(BF16) for h in range(SAMPLE_CHAINS)]
    scores = [_dot_nt(ql, kls[h]) + _dot(qr, rope_buf[slot, :, h * span:(h + 1) * span].astype(BF16))
              for h in range(SAMPLE_CHAINS)]
    for h in range(SAMPLE_CHAINS):
        m, l, acc = _softmax_step(scores[h], kls[h], m_scr[h], l_scr[h], acc_scr[h])
        m_scr[h] = m
        l_scr[h] = l
        acc_scr[h] = acc

    @pl.when(c == n_chunks - 1)
    def _():
        m, l, acc = m_scr[0], l_scr[0], acc_scr[0]
        for h in range(1, SAMPLE_CHAINS):
            m_new = jnp.maximum(m, m_scr[h])
            wa = jnp.exp2(m - m_new)
            wb = jnp.exp2(m_scr[h] - m_new)
            l = wa * l + wb * l_scr[h]
            acc = wa * acc + wb * acc_scr[h]
            m = m_new
        nl = nl_ref[...].astype(BF16)
        s_new = _dot_nt(ql, nl) + _dot_nt(qr, nr_ref[...].astype(BF16))
        tok = lax.shift_right_logical(lax.broadcasted_iota(jnp.int32, (rows, NEW_KEY_PAD), 0),
                                      int(np.log2(N_HEADS)))
        key = lax.broadcasted_iota(jnp.int32, (rows, NEW_KEY_PAD), 1)
        s_new = jnp.where(key <= tok, s_new, MASK_VALUE)
        _, l2, acc2 = _softmax_step(s_new, nl, m, l, acc)
        o_ref[...] = (acc2 / l2).astype(o_ref.dtype)


def _sample_attention(page_table, q_lat, q_rope, new_lat, new_rope, cache_lat, cache_rope_t):
    s, rows, _ = q_lat.shape
    n_pages = page_table.shape[1]
    n = PAGES_PER_STEP
    assert n_pages % n == 0 and rows // N_HEADS <= NEW_KEY_PAD
    seq = lambda r, w: pl.BlockSpec((None, r, w), lambda b, c, pt: (b, 0, 0))
    hbm = pl.BlockSpec(memory_space=pl.ANY)
    grid_spec = pltpu.PrefetchScalarGridSpec(
        num_scalar_prefetch=1,
        grid=(s, n_pages // n),
        in_specs=[seq(rows, KV_LORA), seq(rows, QK_ROPE), seq(NEW_KEY_PAD, KV_LORA), seq(NEW_KEY_PAD, QK_ROPE),
                  hbm, hbm],
        out_specs=seq(rows, KV_LORA),
        scratch_shapes=[pltpu.VMEM((2, n * PAGE_SIZE, KV_LORA), F32),
                        pltpu.VMEM((2, QK_ROPE, n * PAGE_SIZE), F32),
                        pltpu.SemaphoreType.DMA((2, 2)),
                        pltpu.VMEM((SAMPLE_CHAINS, rows, 1), F32),
                        pltpu.VMEM((SAMPLE_CHAINS, rows, 1), F32),
                        pltpu.VMEM((SAMPLE_CHAINS, rows, KV_LORA), F32)],
    )
    return pl.pallas_call(
        _sample_attn_kernel,
        grid_spec=grid_spec,
        out_shape=jax.ShapeDtypeStruct((s, rows, KV_LORA), BF16),
        compiler_params=_params(2),
        name="sample_attention",
    )(page_table, q_lat, q_rope, new_lat, new_rope, cache_lat, cache_rope_t)


def _head_matmul_kernel(x_ref, w_ref, o_ref):
    o_ref[...] = _dot(x_ref[...], w_ref[...]).astype(o_ref.dtype)


def _head_matmul(x, w, *, in_block_stride):
    m = x.shape[0]
    h, xw, yw = w.shape
    return pl.pallas_call(
        _head_matmul_kernel,
        grid=(h,),
        in_specs=[pl.BlockSpec((m, xw), lambda i: (0, i * in_block_stride)),
                  pl.BlockSpec((None, xw, yw), lambda i: (i, 0, 0))],
        out_specs=pl.BlockSpec((m, yw), lambda i: (0, i)),
        out_shape=jax.ShapeDtypeStruct((m, h * yw), BF16),
        compiler_params=_params(1),
        name="head_matmul",
    )(x, w)


def _merge_ln_kernel(o_ref, ga_ref, sg_ref, x_ref, w_ao_ref, w_mix_ref, g_ref, b_ref, h1_ref, *, alpha):
    m = _dot(o_ref[...], w_ao_ref[...])
    mix = (ga_ref[...] + sg_ref[...] * m).astype(BF16)
    z = _dot(mix, w_mix_ref[...])
    h1_ref[...] = _layer_norm(alpha * x_ref[...] + z, g_ref[...], b_ref[...])


def _merge_ln(o, ga, sg, x, w_ao, w_mix, g, b, *, alpha):
    t = x.shape[0]
    tm = min(TOKEN_TILE, t)
    tok = lambda w: pl.BlockSpec((tm, w), lambda i: (i, 0))
    return pl.pallas_call(
        functools.partial(_merge_ln_kernel, alpha=alpha),
        grid=(t // tm,),
        in_specs=[tok(o.shape[1]), tok(D_MODEL), tok(D_MODEL), tok(D_MODEL),
                  _const_spec(w_ao.shape), _const_spec(w_mix.shape),
                  _const_spec(g.shape), _const_spec(b.shape)],
        out_specs=tok(D_MODEL),
        out_shape=jax.ShapeDtypeStruct((t, D_MODEL), F32),
        compiler_params=_params(1),
        name="merge_ln",
    )(o, ga, sg, x, w_ao, w_mix, g, b)


FF_CHUNK = 1024


def _ffn_ln_kernel(h1_ref, w1_ref, w2_ref, g_ref, b_ref, y_ref, *, alpha):
    h1 = h1_ref[...]
    hb = h1.astype(BF16)
    f = jnp.zeros(h1.shape, F32)
    for c in range(0, D_FF, FF_CHUNK):
        hid = jnp.square(jax.nn.relu(_dot(hb, w1_ref[:, c:c + FF_CHUNK]))).astype(BF16)
        f = f + _dot(hid, w2_ref[c:c + FF_CHUNK, :])
    y_ref[...] = _layer_norm(alpha * h1 + f, g_ref[...], b_ref[...])


def _ffn_ln(h1, w1, w2, g, b, *, alpha):
    t = h1.shape[0]
    tm = min(TOKEN_TILE, t)
    tok = pl.BlockSpec((tm, D_MODEL), lambda i: (i, 0))
    return pl.pallas_call(
        functools.partial(_ffn_ln_kernel, alpha=alpha),
        grid=(t // tm,),
        in_specs=[tok, _const_spec(w1.shape), _const_spec(w2.shape),
                  _const_spec(g.shape), _const_spec(b.shape)],
        out_specs=tok,
        out_shape=jax.ShapeDtypeStruct((t, D_MODEL), F32),
        compiler_params=_params(1),
        name="ffn_ln",
    )(h1, w1, w2, g, b)


def _rope_tables(pos):
    freqs = ROPE_THETA ** (-jnp.arange(0, QK_ROPE, 2, dtype=F32) / QK_ROPE)
    ang = pos.astype(F32)[:, None] * freqs[None, :]
    c, s = jnp.cos(ang), jnp.sin(ang)
    z = jnp.zeros_like(c)
    return (jnp.concatenate([c, c, z, z], axis=1),
            jnp.concatenate([-s, z, z, z], axis=1),
            jnp.concatenate([z, s, z, z], axis=1))


def _layer_weights(w_in, conv_w, q_norm_g, w_qb, kv_norm_g, w_kb, w_vb, w_conv_out, w_attn_out,
                   w_mix_out, ln1_g, ln1_b, w_ff1, w_ff2, ln2_g, ln2_b):
    c = D_CONV
    o_q = 3 * c
    o_kv = o_q + Q_LORA
    o_kr = o_kv + KV_LORA
    o_gc = o_kr + QK_ROPE
    o_ga = o_gc + D_MODEL
    d = w_in.shape[0]
    w_in_cg = jnp.concatenate([w_in[:, 0:o_q], w_in[:, o_gc:o_ga + D_MODEL]], axis=1).astype(BF16)
    w_small = jnp.concatenate(
        [w_in[:, o_q:o_gc], jnp.zeros((d, LANES - QK_ROPE), w_in.dtype)], axis=1).astype(BF16)
    wq = w_qb.reshape(Q_LORA, N_HEADS, QK_NOPE + QK_ROPE)
    wq = jnp.concatenate(
        [wq, jnp.zeros((Q_LORA, N_HEADS, HEAD_PAD - QK_NOPE - QK_ROPE), w_qb.dtype)], axis=2)
    return dict(
        w_in_cg=w_in_cg, w_small=w_small, conv_w=conv_w,
        qg=q_norm_g[None, :], kvg=kv_norm_g[None, :],
        w_qb=wq.reshape(Q_LORA, N_HEADS * HEAD_PAD).astype(BF16),
        w_kb=w_kb.reshape(KV_LORA, N_HEADS * QK_NOPE).astype(BF16),
        w_vb_t=w_vb.reshape(KV_LORA, N_HEADS * V_DIM).T.astype(BF16),
        w_kb_heads=jnp.transpose(w_kb, (1, 2, 0)).astype(BF16),
        w_vb_heads=jnp.transpose(w_vb, (1, 0, 2)).astype(BF16),
        w_co=w_conv_out.astype(BF16), w_ao=w_attn_out.astype(BF16), w_mix=w_mix_out.astype(BF16),
        ln1_g=ln1_g[None, :], ln1_b=ln1_b[None, :],
        w_ff1=w_ff1.astype(BF16), w_ff2=w_ff2.astype(BF16),
        ln2_g=ln2_g[None, :], ln2_b=ln2_b[None, :],
    )


def _tail(o, ga, sg, x, lw, alpha):
    h1 = _merge_ln(o, ga, sg, x, lw["w_ao"], lw["w_mix"], lw["ln1_g"], lw["ln1_b"], alpha=alpha)
    return _ffn_ln(h1, lw["w_ff1"], lw["w_ff2"], lw["ln2_g"], lw["ln2_b"], alpha=alpha)


def _prompt_layer(x, conv_prev, lw, alpha):
    b, t, d = x.shape
    ga, sg, conv_new = _conv_gate(x, conv_prev, lw["w_in_cg"], lw["conv_w"], lw["w_co"], sample_rows=None)
    q, k, v_t, lat, krope = _attn_front(x, _rope_tables(jnp.arange(t)), lw["w_small"], lw["qg"], lw["kvg"],
                                        lw["w_qb"], lw["w_kb"], lw["w_vb_t"])
    o = _prompt_attention(q, k, v_t)
    y = _tail(o.reshape(b * t, -1), ga.reshape(b * t, d), sg.reshape(b * t, d), x.reshape(b * t, d), lw, alpha)
    return y.reshape(b, t, d), lat, krope, conv_new


def _sample_layer(x, conv_prev, cache_lat, cache_rope, page_table, lw, alpha):
    s, t, d = x.shape
    past = page_table.shape[1] * PAGE_SIZE
    xt = jnp.transpose(x, (1, 0, 2)).reshape(1, t * s, d)
    state = jnp.transpose(conv_prev, (1, 0, 2)).reshape(1, (CONV_W - 1) * s, D_CONV)
    ga, sg, conv_new = _conv_gate(xt, state, lw["w_in_cg"], lw["conv_w"], lw["w_co"], sample_rows=s)
    tables = _rope_tables(jnp.repeat(past + jnp.arange(t), s))
    q, _, _, lat, krope = _attn_front(xt, tables, lw["w_small"], lw["qg"], lw["kvg"],
                                      lw["w_qb"], lw["w_kb"], lw["w_vb_t"])
    q = q[0]
    q_lat = _head_matmul(q, lw["w_kb_heads"], in_block_stride=HEAD_PAD // QK_NOPE)

    def per_seq(a, w):
        return jnp.transpose(a.reshape(t, s, N_HEADS, w), (1, 0, 2, 3)).reshape(s, t * N_HEADS, w)

    q_rope = per_seq(q.reshape(t * s, N_HEADS, HEAD_PAD)[:, :, QK_NOPE:QK_NOPE + QK_ROPE].reshape(t * s, -1), QK_ROPE)
    lat_seq = jnp.transpose(lat.reshape(t, s, KV_LORA), (1, 0, 2))
    krope_seq = jnp.transpose(krope.reshape(t, s, QK_ROPE), (1, 0, 2))
    pad = ((0, 0), (0, NEW_KEY_PAD - t), (0, 0))
    o_lat = _sample_attention(page_table, per_seq(q_lat, KV_LORA), q_rope,
                              jnp.pad(lat_seq, pad), jnp.pad(krope_seq, pad),
                              cache_lat, jnp.swapaxes(cache_rope, 1, 2))
    o_lat = jnp.transpose(o_lat.reshape(s, t, N_HEADS, KV_LORA), (1, 0, 2, 3)).reshape(t * s, -1)
    o = _head_matmul(o_lat, lw["w_vb_heads"], in_block_stride=1)
    y = _tail(o, ga[0], sg[0], xt[0], lw, alpha)
    y = jnp.transpose(y.reshape(t, s, d), (1, 0, 2))
    conv_new = jnp.transpose(conv_new.reshape(CONV_W - 1, s, D_CONV), (1, 0, 2))
    return y, lat_seq, krope_seq, conv_new


def kernel(x_prompt, x_sample, cache_latent, cache_krope, state_conv, page_table, w_in, conv_w, q_norm_g, w_qb, kv_norm_g, w_kb, w_vb, w_conv_out, w_attn_out, w_mix_out, ln1_g, ln1_b, w_ff1, w_ff2, ln2_g, ln2_b):
    depth = w_in.shape[0]
    alpha = (2 * depth) ** 0.25
    conv_zero = jnp.zeros((x_prompt.shape[0], CONV_W - 1, D_CONV), x_prompt.dtype)
    y_p, y_s = x_prompt, x_sample
    outs = [[] for _ in range(6)]
    for l in range(depth):
        lw = _layer_weights(w_in[l], conv_w[l], q_norm_g[l], w_qb[l], kv_norm_g[l], w_kb[l], w_vb[l],
                            w_conv_out[l], w_attn_out[l], w_mix_out[l], ln1_g[l], ln1_b[l],
                            w_ff1[l], w_ff2[l], ln2_g[l], ln2_b[l])
        y_p, lp, rp, cp = _prompt_layer(y_p, conv_zero, lw, alpha)
        y_s, ls, rs, cs = _sample_layer(y_s, state_conv[l], cache_latent[l], cache_krope[l], page_table, lw, alpha)
        for acc, val in zip(outs, (lp, rp, cp, ls, rs, cs)):
            acc.append(val)
    return (y_p, y_s) + tuple(jnp.stack(o) for o in outs)
```

```python
import functools

import jax
import jax.numpy as jnp
import numpy as np
from jax import lax
from jax.experimental import pallas as pl
from jax.experimental.pallas import tpu as pltpu

F32 = jnp.float32
BF16 = jnp.bfloat16

D_MODEL = 1024
D_CONV = 1024
CONV_W = 3
N_HEADS = 16
QK_NOPE = 128
QK_ROPE = 64
V_DIM = 128
Q_LORA = 256
KV_LORA = 256
ROPE_THETA = 10000.0
D_FF = 4 * D_MODEL
NORM_EPS = 1e-5
PAGE_SIZE = 128
MASK_VALUE = -1e30
ATTN_SCALE = (QK_NOPE + QK_ROPE) ** -0.5

LANES = 128
HEAD_PAD = 2 * LANES
SMALL_W = Q_LORA + KV_LORA + LANES
VMEM_LIMIT = 56 * 1024 * 1024

QK_LOG2_SCALE = ATTN_SCALE * float(np.log2(np.e))

TOKEN_TILE = 512
ATTN_Q_TILE = 512
ATTN_HEADS_PER_STEP = 8
REDUCE_FANIN = 8
PAGES_PER_STEP = 32
SAMPLE_CHAINS = 4
NEW_KEY_PAD = 16


def _dot(a, b):
    return jnp.dot(a, b, preferred_element_type=F32)


def _dot_nt(a, b):
    return lax.dot_general(a, b, (((1,), (1,)), ((), ())), preferred_element_type=F32)


def _const_spec(shape):
    zeros = (0,) * len(shape)
    return pl.BlockSpec(shape, lambda *_: zeros, pipeline_mode=pl.Buffered(1))


def _params(n_grid_dims, flags=None):
    return pltpu.CompilerParams(
        dimension_semantics=("arbitrary",) * n_grid_dims,
        vmem_limit_bytes=VMEM_LIMIT,
        flags=flags,
    )


def _rms_norm(x, g):
    ms = jnp.mean(jnp.square(x), axis=-1, keepdims=True)
    return x * lax.rsqrt(ms + NORM_EPS) * g


def _layer_norm(x, g, b):
    mu = jnp.mean(x, axis=-1, keepdims=True)
    xc = x - mu
    var = jnp.mean(jnp.square(xc), axis=-1, keepdims=True)
    return xc * lax.rsqrt(var + NORM_EPS) * g + b


def _rope_padded(r, cos, sin_lo, sin_hi):
    return (r * cos
            + pltpu.roll(r, LANES - QK_ROPE // 2, axis=1) * sin_lo
            + pltpu.roll(r, QK_ROPE // 2, axis=1) * sin_hi)


def _conv_gate_kernel(x_ref, state_ref, w_in_ref, conv_w_ref, w_co_ref,
                      ga_ref, sg_ref, conv_new_ref, u_scr, *, sample_rows):
    tm = x_ref.shape[0]
    xb = x_ref[...].astype(BF16)
    c = D_CONV
    u = _dot(xb, w_in_ref[:, c:2 * c]) * _dot(xb, w_in_ref[:, 2 * c:3 * c])
    w0 = conv_w_ref[0:1, :]
    w1 = conv_w_ref[1:2, :]
    w2 = conv_w_ref[2:3, :]
    if sample_rows is None:
        @pl.when(pl.program_id(1) == 0)
        def _():
            u_scr[0:6, :] = jnp.zeros((6, c), F32)
            u_scr[6:8, :] = state_ref[...]
        u_scr[8:8 + tm, :] = u
        v = w2 * u + w1 * u_scr[7:7 + tm, :] + w0 * u_scr[6:6 + tm, :]
        tail = u_scr[tm:tm + 8, :]
        u_scr[0:8, :] = tail
        conv_new_ref[...] = tail[6:8, :]
    else:
        s = sample_rows
        u_scr[0:2 * s, :] = state_ref[...]
        u_scr[2 * s:2 * s + tm, :] = u
        v = w2 * u + w1 * u_scr[s:s + tm, :] + w0 * u_scr[0:tm, :]
        conv_new_ref[...] = u_scr[tm:tm + 2 * s, :]
    a_in = (_dot(xb, w_in_ref[:, 0:c]) * v).astype(BF16)
    a = _dot(a_in, w_co_ref[...])
    ga_ref[...] = jax.nn.sigmoid(_dot(xb, w_in_ref[:, 3 * c:3 * c + D_MODEL])) * a
    sg_ref[...] = jax.nn.sigmoid(_dot(xb, w_in_ref[:, 3 * c + D_MODEL:3 * c + 2 * D_MODEL]))


def _conv_gate(x, state, w_in_cg, conv_w, w_co, *, sample_rows):
    g, t, d = x.shape
    tm = min(TOKEN_TILE, t)
    n_state = state.shape[1]
    scr_rows = 8 + tm if sample_rows is None else 2 * sample_rows + tm
    tok = lambda w: pl.BlockSpec((None, tm, w), lambda b, i: (b, i, 0))
    return pl.pallas_call(
        functools.partial(_conv_gate_kernel, sample_rows=sample_rows),
        grid=(g, t // tm),
        in_specs=[tok(d),
                  pl.BlockSpec((None, n_state, D_CONV), lambda b, i: (b, 0, 0)),
                  _const_spec(w_in_cg.shape), _const_spec(conv_w.shape), _const_spec(w_co.shape)],
        out_specs=[tok(D_MODEL), tok(D_MODEL),
                   pl.BlockSpec((None, n_state, D_CONV), lambda b, i: (b, 0, 0))],
        out_shape=[jax.ShapeDtypeStruct((g, t, D_MODEL), F32),
                   jax.ShapeDtypeStruct((g, t, D_MODEL), F32),
                   jax.ShapeDtypeStruct((g, n_state, D_CONV), F32)],
        scratch_shapes=[pltpu.VMEM((scr_rows, D_CONV), F32)],
        compiler_params=_params(2),
        name="conv_gate",
    )(x, state, w_in_cg, conv_w, w_co)


def _attn_front_kernel(x_ref, cos_ref, sin_lo_ref, sin_hi_ref, w_small_ref, qg_ref, kvg_ref,
                       w_qb_ref, w_kb_ref, w_vb_ref,
                       q_ref, k_ref, vt_ref, lat_ref, krope_ref):
    xb = x_ref[...].astype(BF16)
    small = _dot(xb, w_small_ref[...])
    cos = cos_ref[...]
    sin_lo = sin_lo_ref[...]
    sin_hi = sin_hi_ref[...]

    qn = _rms_norm(small[:, 0:Q_LORA], qg_ref[...]).astype(BF16)
    q = _dot(qn, w_qb_ref[...]) * QK_LOG2_SCALE
    for h in range(N_HEADS):
        lo = h * HEAD_PAD
        q_ref[:, lo:lo + LANES] = q[:, lo:lo + LANES].astype(BF16)
        q_ref[:, lo + LANES:lo + HEAD_PAD] = _rope_padded(
            q[:, lo + LANES:lo + HEAD_PAD], cos, sin_lo, sin_hi).astype(BF16)

    lat = _rms_norm(small[:, Q_LORA:Q_LORA + KV_LORA], kvg_ref[...])
    lat_ref[...] = lat
    latb = lat.astype(BF16)
    krope = _rope_padded(small[:, Q_LORA + KV_LORA:SMALL_W], cos, sin_lo, sin_hi)
    krope_ref[...] = krope[:, 0:QK_ROPE]
    kropeb = krope.astype(BF16)
    kn = _dot(latb, w_kb_ref[...]).astype(BF16)
    for h in range(N_HEADS):
        lo = h * HEAD_PAD
        k_ref[:, lo:lo + LANES] = kn[:, h * QK_NOPE:(h + 1) * QK_NOPE]
        k_ref[:, lo + LANES:lo + HEAD_PAD] = kropeb
    vt_ref[...] = _dot_nt(w_vb_ref[...], latb).astype(BF16)


def _attn_front(x, tables, w_small, qg, kvg, w_qb, w_kb, w_vb_t):
    g, t, d = x.shape
    tm = min(TOKEN_TILE, t)
    tok = lambda w: pl.BlockSpec((None, tm, w), lambda b, i: (b, i, 0))
    tab = pl.BlockSpec((tm, LANES), lambda b, i: (i, 0))
    return pl.pallas_call(
        _attn_front_kernel,
        grid=(g, t // tm),
        in_specs=[tok(d), tab, tab, tab,
                  _const_spec(w_small.shape), _const_spec(qg.shape), _const_spec(kvg.shape),
                  _const_spec(w_qb.shape), _const_spec(w_kb.shape), _const_spec(w_vb_t.shape)],
        out_specs=[tok(N_HEADS * HEAD_PAD), tok(N_HEADS * HEAD_PAD),
                   pl.BlockSpec((None, None, N_HEADS * V_DIM, tm), lambda b, i: (b, i, 0, 0)),
                   tok(KV_LORA), tok(QK_ROPE)],
        out_shape=[jax.ShapeDtypeStruct((g, t, N_HEADS * HEAD_PAD), BF16),
                   jax.ShapeDtypeStruct((g, t, N_HEADS * HEAD_PAD), BF16),
                   jax.ShapeDtypeStruct((g, t // tm, N_HEADS * V_DIM, tm), BF16),
                   jax.ShapeDtypeStruct((g, t, KV_LORA), F32),
                   jax.ShapeDtypeStruct((g, t, QK_ROPE), F32)],
        compiler_params=_params(2),
        name="attn_front",
    )(x, *tables, w_small, qg, kvg, w_qb, w_kb, w_vb_t)


def _softmax_step(s, v, m, l, acc):
    m_new = jnp.maximum(m, jnp.max(s, axis=-1, keepdims=True))
    alpha = jnp.exp2(m - m_new)
    p = jnp.exp2(s - m_new)
    l = alpha * l + jnp.sum(p, axis=-1, keepdims=True)
    acc = alpha * acc + _dot(p.astype(BF16), v)
    return m_new, l, acc


def _softmax_init(rows, width):
    return (jnp.full((rows, 1), MASK_VALUE, F32), jnp.zeros((rows, 1), F32), jnp.zeros((rows, width), F32))


def _prompt_attn_kernel(q_ref, k_ref, vt_ref, o_ref):
    tq = q_ref.shape[0]
    tk = vt_ref.shape[2]
    qi = pl.program_id(2)
    heads = range(q_ref.shape[1] // HEAD_PAD)
    qs = [q_ref[:, h * HEAD_PAD:(h + 1) * HEAD_PAD] for h in heads]
    n_full = (qi * tq) // tk

    def reduce_keys(x, op):
        part = op(x.reshape(REDUCE_FANIN, tk // REDUCE_FANIN, tq), axis=0)
        return op(part, axis=0, keepdims=True)

    def chunk(j, carry, diagonal=False):
        start = pl.multiple_of(j * tk, tk)

        def scores(h):
            return _dot_nt(k_ref[pl.ds(start, tk), h * HEAD_PAD:(h + 1) * HEAD_PAD], qs[h])

        def softmax(h, s_t):
            m, l, _ = carry[h]
            if diagonal:
                key = j * tk + lax.broadcasted_iota(jnp.int32, (tk, tq), 0)
                qry = qi * tq + lax.broadcasted_iota(jnp.int32, (tk, tq), 1)
                s_t = jnp.where(key <= qry, s_t, MASK_VALUE)
            m_new = jnp.maximum(m, reduce_keys(s_t, jnp.max))
            alpha = jnp.exp2(m - m_new)
            p_t = jnp.exp2(s_t - m_new)
            return m_new, alpha * l + reduce_keys(p_t, jnp.sum), alpha, p_t.astype(BF16)

        def values(h, alpha, p_t):
            return alpha * carry[h][2] + _dot(vt_ref[j, h * V_DIM:(h + 1) * V_DIM, :], p_t)

        s = [scores(h) for h in heads]
        out = []
        for h in heads:
            m_new, l, alpha, p_t = softmax(h, s[h])
            out.append((m_new, l, values(h, alpha, p_t)))
        return tuple(out)

    init = tuple((jnp.full((1, tq), MASK_VALUE, F32), jnp.zeros((1, tq), F32), jnp.zeros((V_DIM, tq), F32))
                 for _ in heads)
    carry = lax.fori_loop(0, n_full, chunk, init)
    for h, (_, l, acc_t) in enumerate(chunk(n_full, carry, diagonal=True)):
        o_ref[:, h * V_DIM:(h + 1) * V_DIM] = jnp.transpose(acc_t / l).astype(o_ref.dtype)


def _prompt_attention(q, k, v_t):
    b, t, _ = q.shape
    tq = ATTN_Q_TILE
    g = ATTN_HEADS_PER_STEP
    n_k, _, tk = v_t.shape[1:]
    assert tk % tq == 0 and t == n_k * tk and N_HEADS % g == 0
    return pl.pallas_call(
        _prompt_attn_kernel,
        grid=(b, N_HEADS // g, t // tq),
        in_specs=[pl.BlockSpec((None, tq, g * HEAD_PAD), lambda bi, h, i: (bi, i, h)),
                  pl.BlockSpec((None, t, g * HEAD_PAD), lambda bi, h, i: (bi, 0, h),
                               pipeline_mode=pl.Buffered(1)),
                  pl.BlockSpec((None, n_k, g * V_DIM, tk), lambda bi, h, i: (bi, 0, h, 0),
                               pipeline_mode=pl.Buffered(1))],
        out_specs=pl.BlockSpec((None, tq, g * V_DIM), lambda bi, h, i: (bi, i, h)),
        out_shape=jax.ShapeDtypeStruct((b, t, N_HEADS * V_DIM), BF16),
        compiler_params=_params(3),
        name="prompt_attention",
    )(q, k, v_t)


def _sample_attn_kernel(pt_ref, ql_ref, qr_ref, nl_ref, nr_ref, lat_hbm, ropet_hbm,
                        o_ref, lat_buf, rope_buf, sems, m_scr, l_scr, acc_scr):
    n = PAGES_PER_STEP
    b = pl.program_id(0)
    c = pl.program_id(1)
    n_chunks = pl.num_programs(1)
    step = b * n_chunks + c
    n_steps = pl.num_programs(0) * n_chunks
    slot = lax.rem(step, 2)
    rows = ql_ref.shape[0]

    def page_copies(bb, cc, sl):
        copies = []
        for j in range(n):
            page = pt_ref[bb, cc * n + j]
            keys = pl.ds(j * PAGE_SIZE, PAGE_SIZE)
            copies.append(pltpu.make_async_copy(lat_hbm.at[page], lat_buf.at[sl, keys, :], sems.at[0, sl]))
            copies.append(pltpu.make_async_copy(ropet_hbm.at[page], rope_buf.at[sl, :, keys], sems.at[1, sl]))
        return copies

    def start_all(copies):
        for i, cp in enumerate(copies):
            cp.start(priority=(i // 2) % 2)

    @pl.when(step == 0)
    def _():
        start_all(page_copies(b, c, slot))

    @pl.when(step + 1 < n_steps)
    def _():
        wrap = c + 1 == n_chunks
        start_all(page_copies(jnp.where(wrap, b + 1, b), jnp.where(wrap, 0, c + 1), 1 - slot))

    @pl.when(c == 0)
    def _():
        m_scr[...] = jnp.full(m_scr.shape, MASK_VALUE, F32)
        l_scr[...] = jnp.zeros(l_scr.shape, F32)
        acc_scr[...] = jnp.zeros(acc_scr.shape, F32)

    for cp in page_copies(b, c, slot):
        cp.wait()

    ql = ql_ref[...]
    qr = qr_ref[...]
    span = n * PAGE_SIZE // SAMPLE_CHAINS
    kls = [lat_buf[slot, h * span:(h + 1) * span, :].astype(BF16) for h in range(SAMPLE_CHAINS)]
    scores = [_dot_nt(ql, kls[h]) + _dot(qr, rope_buf[slot, :, h * span:(h + 1) * span].astype(BF16))
              for h in range(SAMPLE_CHAINS)]
    for h in range(SAMPLE_CHAINS):
        m, l, acc = _softmax_step(scores[h], kls[h], m_scr[h], l_scr[h], acc_scr[h])
        m_scr[h] = m
        l_scr[h] = l
        acc_scr[h] = acc

    @pl.when(c == n_chunks - 1)
    def _():
        m, l, acc = m_scr[0], l_scr[0], acc_scr[0]
        for h in range(1, SAMPLE_CHAINS):
            m_new = jnp.maximum(m, m_scr[h])
            wa = jnp.exp2(m - m_new)
            wb = jnp.exp2(m_scr[h] - m_new)
            l = wa * l + wb * l_scr[h]
            acc = wa * acc + wb * acc_scr[h]
            m = m_new
        nl = nl_ref[...].astype(BF16)
        s_new = _dot_nt(ql, nl) + _dot_nt(qr, nr_ref[...].astype(BF16))
        tok = lax.shift_right_logical(lax.broadcasted_iota(jnp.int32, (rows, NEW_KEY_PAD), 0),
                                      int(np.log2(N_HEADS)))
        key = lax.broadcasted_iota(jnp.int32, (rows, NEW_KEY_PAD), 1)
        s_new = jnp.where(key <= tok, s_new, MASK_VALUE)
        _, l2, acc2 = _softmax_step(s_new, nl, m, l, acc)
        o_ref[...] = (acc2 / l2).astype(o_ref.dtype)


def _sample_attention(page_table, q_lat, q_rope, new_lat, new_rope, cache_lat, cache_rope_t):
    s, rows, _ = q_lat.shape
    n_pages = page_table.shape[1]
    n = PAGES_PER_STEP
    assert n_pages % n == 0 and rows // N_HEADS <= NEW_KEY_PAD
    seq = lambda r, w: pl.BlockSpec((None, r, w), lambda b, c, pt: (b, 0, 0))
    hbm = pl.BlockSpec(memory_space=pl.ANY)
    grid_spec = pltpu.PrefetchScalarGridSpec(
        num_scalar_prefetch=1,
        grid=(s, n_pages // n),
        in_specs=[seq(rows, KV_LORA), seq(rows, QK_ROPE), seq(NEW_KEY_PAD, KV_LORA), seq(NEW_KEY_PAD, QK_ROPE),
                  hbm, hbm],
        out_specs=seq(rows, KV_LORA),
        scratch_shapes=[pltpu.VMEM((2, n * PAGE_SIZE, KV_LORA), F32),
                        pltpu.VMEM((2, QK_ROPE, n * PAGE_SIZE), F32),
                        pltpu.SemaphoreType.DMA((2, 2)),
                        pltpu.VMEM((SAMPLE_CHAINS, rows, 1), F32),
                        pltpu.VMEM((SAMPLE_CHAINS, rows, 1), F32),
                        pltpu.VMEM((SAMPLE_CHAINS, rows, KV_LORA), F32)],
    )
    return pl.pallas_call(
        _sample_attn_kernel,
        grid_spec=grid_spec,
        out_shape=jax.ShapeDtypeStruct((s, rows, KV_LORA), BF16),
        compiler_params=_params(2),
        name="sample_attention",
    )(page_table, q_lat, q_rope, new_lat, new_rope, cache_lat, cache_rope_t)


def _head_matmul_kernel(x_ref, w_ref, o_ref):
    o_ref[...] = _dot(x_ref[...], w_ref[...]).astype(o_ref.dtype)


def _head_matmul(x, w, *, in_block_stride):
    m = x.shape[0]
    h, xw, yw = w.shape
    return pl.pallas_call(
        _head_matmul_kernel,
        grid=(h,),
        in_specs=[pl.BlockSpec((m, xw), lambda i: (0, i * in_block_stride)),
                  pl.BlockSpec((None, xw, yw), lambda i: (i, 0, 0))],
        out_specs=pl.BlockSpec((m, yw), lambda i: (0, i)),
        out_shape=jax.ShapeDtypeStruct((m, h * yw), BF16),
        compiler_params=_params(1),
        name="head_matmul",
    )(x, w)


def _merge_ln_kernel(o_ref, ga_ref, sg_ref, x_ref, w_ao_ref, w_mix_ref, g_ref, b_ref, h1_ref, *, alpha):
    m = _dot(o_ref[...], w_ao_ref[...])
    mix = (ga_ref[...] + sg_ref[...] * m).astype(BF16)
    z = _dot(mix, w_mix_ref[...])
    h1_ref[...] = _layer_norm(alpha * x_ref[...] + z, g_ref[...], b_ref[...])


def _merge_ln(o, ga, sg, x, w_ao, w_mix, g, b, *, alpha):
    t = x.shape[0]
    tm = min(TOKEN_TILE, t)
    tok = lambda w: pl.BlockSpec((tm, w), lambda i: (i, 0))
    return pl.pallas_call(
        functools.partial(_merge_ln_kernel, alpha=alpha),
        grid=(t // tm,),
        in_specs=[tok(o.shape[1]), tok(D_MODEL), tok(D_MODEL), tok(D_MODEL),
                  _const_spec(w_ao.shape), _const_spec(w_mix.shape),
                  _const_spec(g.shape), _const_spec(b.shape)],
        out_specs=tok(D_MODEL),
        out_shape=jax.ShapeDtypeStruct((t, D_MODEL), F32),
        compiler_params=_params(1),
        name="merge_ln",
    )(o, ga, sg, x, w_ao, w_mix, g, b)


FF_CHUNK = 1024


def _ffn_ln_kernel(h1_ref, w1_ref, w2_ref, g_ref, b_ref, y_ref, *, alpha):
    h1 = h1_ref[...]
    hb = h1.astype(BF16)
    f = jnp.zeros(h1.shape, F32)
    for c in range(0, D_FF, FF_CHUNK):
        hid = jnp.square(jax.nn.relu(_dot(hb, w1_ref[:, c:c + FF_CHUNK]))).astype(BF16)
        f = f + _dot(hid, w2_ref[c:c + FF_CHUNK, :])
    y_ref[...] = _layer_norm(alpha * h1 + f, g_ref[...], b_ref[...])


def _ffn_ln(h1, w1, w2, g, b, *, alpha):
    t = h1.shape[0]
    tm = min(TOKEN_TILE, t)
    tok = pl.BlockSpec((tm, D_MODEL), lambda i: (i, 0))
    return pl.pallas_call(
        functools.partial(_ffn_ln_kernel, alpha=alpha),
        grid=(t // tm,),
        in_specs=[tok, _const_spec(w1.shape), _const_spec(w2.shape),
                  _const_spec(g.shape), _const_spec(b.shape)],
        out_specs=tok,
        out_shape=jax.ShapeDtypeStruct((t, D_MODEL), F32),
        compiler_params=_params(1),
        name="ffn_ln",
    )(h1, w1, w2, g, b)


def _rope_tables(pos):
    freqs = ROPE_THETA ** (-jnp.arange(0, QK_ROPE, 2, dtype=F32) / QK_ROPE)
    ang = pos.astype(F32)[:, None] * freqs[None, :]
    c, s = jnp.cos(ang), jnp.sin(ang)
    z = jnp.zeros_like(c)
    return (jnp.concatenate([c, c, z, z], axis=1),
            jnp.concatenate([-s, z, z, z], axis=1),
            jnp.concatenate([z, s, z, z], axis=1))


def _layer_weights(w_in, conv_w, q_norm_g, w_qb, kv_norm_g, w_kb, w_vb, w_conv_out, w_attn_out,
                   w_mix_out, ln1_g, ln1_b, w_ff1, w_ff2, ln2_g, ln2_b):
    c = D_CONV
    o_q = 3 * c
    o_kv = o_q + Q_LORA
    o_kr = o_kv + KV_LORA
    o_gc = o_kr + QK_ROPE
    o_ga = o_gc + D_MODEL
    d = w_in.shape[0]
    w_in_cg = jnp.concatenate([w_in[:, 0:o_q], w_in[:, o_gc:o_ga + D_MODEL]], axis=1).astype(BF16)
    w_small = jnp.concatenate(
        [w_in[:, o_q:o_gc], jnp.zeros((d, LANES - QK_ROPE), w_in.dtype)], axis=1).astype(BF16)
    wq = w_qb.reshape(Q_LORA, N_HEADS, QK_NOPE + QK_ROPE)
    wq = jnp.concatenate(
        [wq, jnp.zeros((Q_LORA, N_HEADS, HEAD_PAD - QK_NOPE - QK_ROPE), w_qb.dtype)], axis=2)
    return dict(
        w_in_cg=w_in_cg, w_small=w_small, conv_w=conv_w,
        qg=q_norm_g[None, :], kvg=kv_norm_g[None, :],
        w_qb=wq.reshape(Q_LORA, N_HEADS * HEAD_PAD).astype(BF16),
        w_kb=w_kb.reshape(KV_LORA, N_HEADS * QK_NOPE).astype(BF16),
        w_vb_t=w_vb.reshape(KV_LORA, N_HEADS * V_DIM).T.astype(BF16),
        w_kb_heads=jnp.transpose(w_kb, (1, 2, 0)).astype(BF16),
        w_vb_heads=jnp.transpose(w_vb, (1, 0, 2)).astype(BF16),
        w_co=w_conv_out.astype(BF16), w_ao=w_attn_out.astype(BF16), w_mix=w_mix_out.astype(BF16),
        ln1_g=ln1_g[None, :], ln1_b=ln1_b[None, :],
        w_ff1=w_ff1.astype(BF16), w_ff2=w_ff2.astype(BF16),
        ln2_g=ln2_g[None, :], ln2_b=ln2_b[None, :],
    )


def _tail(o, ga, sg, x, lw, alpha):
    h1 = _merge_ln(o, ga, sg, x, lw["w_ao"], lw["w_mix"], lw["ln1_g"], lw["ln1_b"], alpha=alpha)
    return _ffn_ln(h1, lw["w_ff1"], lw["w_ff2"], lw["ln2_g"], lw["ln2_b"], alpha=alpha)


def _prompt_layer(x, conv_prev, lw, alpha):
    b, t, d = x.shape
    ga, sg, conv_new = _conv_gate(x, conv_prev, lw["w_in_cg"], lw["conv_w"], lw["w_co"], sample_rows=None)
    q, k, v_t, lat, krope = _attn_front(x, _rope_tables(jnp.arange(t)), lw["w_small"], lw["qg"], lw["kvg"],
                                        lw["w_qb"], lw["w_kb"], lw["w_vb_t"])
    o = _prompt_attention(q, k, v_t)
    y = _tail(o.reshape(b * t, -1), ga.reshape(b * t, d), sg.reshape(b * t, d), x.reshape(b * t, d), lw, alpha)
    return y.reshape(b, t, d), lat, krope, conv_new


def _sample_layer(x, conv_prev, cache_lat, cache_rope, page_table, lw, alpha):
    s, t, d = x.shape
    past = page_table.shape[1] * PAGE_SIZE
    xt = jnp.transpose(x, (1, 0, 2)).reshape(1, t * s, d)
    state = jnp.transpose(conv_prev, (1, 0, 2)).reshape(1, (CONV_W - 1) * s, D_CONV)
    ga, sg, conv_new = _conv_gate(xt, state, lw["w_in_cg"], lw["conv_w"], lw["w_co"], sample_rows=s)
    tables = _rope_tables(jnp.repeat(past + jnp.arange(t), s))
    q, _, _, lat, krope = _attn_front(xt, tables, lw["w_small"], lw["qg"], lw["kvg"],
                                      lw["w_qb"], lw["w_kb"], lw["w_vb_t"])
    q = q[0]
    q_lat = _head_matmul(q, lw["w_kb_heads"], in_block_stride=HEAD_PAD // QK_NOPE)

    def per_seq(a, w):
        return jnp.transpose(a.reshape(t, s, N_HEADS, w), (1, 0, 2, 3)).reshape(s, t * N_HEADS, w)

    q_rope = per_seq(q.reshape(t * s, N_HEADS, HEAD_PAD)[:, :, QK_NOPE:QK_NOPE + QK_ROPE].reshape(t * s, -1), QK_ROPE)
    lat_seq = jnp.transpose(lat.reshape(t, s, KV_LORA), (1, 0, 2))
    krope_seq = jnp.transpose(krope.reshape(t, s, QK_ROPE), (1, 0, 2))
    pad = ((0, 0), (0, NEW_KEY_PAD - t), (0, 0))
    o_lat = _sample_attention(page_table, per_seq(q_lat, KV_LORA), q_rope,
                              jnp.pad(lat_seq, pad), jnp.pad(krope_seq, pad),
                              cache_lat, jnp.swapaxes(cache_rope, 1, 2))
    o_lat = jnp.transpose(o_lat.reshape(s, t, N_HEADS, KV_LORA), (1, 0, 2, 3)).reshape(t * s, -1)
    o = _head_matmul(o_lat, lw["w_vb_heads"], in_block_stride=1)
    y = _tail(o, ga[0], sg[0], xt[0], lw, alpha)
    y = jnp.transpose(y.reshape(t, s, d), (1, 0, 2))
    conv_new = jnp.transpose(conv_new.reshape(CONV_W - 1, s, D_CONV), (1, 0, 2))
    return y, lat_seq, krope_seq, conv_new


def kernel(x_prompt, x_sample, cache_latent, cache_krope, state_conv, page_table, w_in, conv_w, q_norm_g, w_qb, kv_norm_g, w_kb, w_vb, w_conv_out, w_attn_out, w_mix_out, ln1_g, ln1_b, w_ff1, w_ff2, ln2_g, ln2_b):
    depth = w_in.shape[0]
    alpha = (2 * depth) ** 0.25
    conv_zero = jnp.zeros((x_prompt.shape[0], CONV_W - 1, D_CONV), x_prompt.dtype)
    y_p, y_s = x_prompt, x_sample
    outs = [[] for _ in range(6)]
    for l in range(depth):
        lw = _layer_weights(w_in[l], conv_w[l], q_norm_g[l], w_qb[l], kv_norm_g[l], w_kb[l], w_vb[l],
                            w_conv_out[l], w_attn_out[l], w_mix_out[l], ln1_g[l], ln1_b[l],
                            w_ff1[l], w_ff2[l], ln2_g[l], ln2_b[l])
        y_p, lp, rp, cp = _prompt_layer(y_p, conv_zero, lw, alpha)
        y_s, ls, rs, cs = _sample_layer(y_s, state_conv[l], cache_latent[l], cache_krope[l], page_table, lw, alpha)
        for acc, val in zip(outs, (lp, rp, cp, ls, rs, cs)):
            acc.append(val)
    return (y_p, y_s) + tuple(jnp.stack(o) for o in outs)
```

```python
import functools

import jax
import jax.numpy as jnp
import numpy as np
from jax import lax
from jax.experimental import pallas as pl
from jax.experimental.pallas import tpu as pltpu

F32 = jnp.float32
BF16 = jnp.bfloat16

D_MODEL = 1024
D_CONV = 1024
CONV_W = 3
N_HEADS = 16
QK_NOPE = 128
QK_ROPE = 64
V_DIM = 128
Q_LORA = 256
KV_LORA = 256
ROPE_THETA = 10000.0
D_FF = 4 * D_MODEL
NORM_EPS = 1e-5
PAGE_SIZE = 128
MASK_VALUE = -1e30
ATTN_SCALE = (QK_NOPE + QK_ROPE) ** -0.5

LANES = 128
HEAD_PAD = 2 * LANES
SMALL_W = Q_LORA + KV_LORA + LANES
VMEM_LIMIT = 56 * 1024 * 1024

QK_LOG2_SCALE = ATTN_SCALE * float(np.log2(np.e))

TOKEN_TILE = 512
ATTN_Q_TILE = 512
ATTN_HEADS_PER_STEP = 8
REDUCE_FANIN = 8
PAGES_PER_STEP = 32
SAMPLE_CHAINS = 4
NEW_KEY_PAD = 16


def _dot(a, b):
    return jnp.dot(a, b, preferred_element_type=F32)


def _dot_nt(a, b):
    return lax.dot_general(a, b, (((1,), (1,)), ((), ())), preferred_element_type=F32)


def _const_spec(shape):
    zeros = (0,) * len(shape)
    return pl.BlockSpec(shape, lambda *_: zeros, pipeline_mode=pl.Buffered(1))


def _params(n_grid_dims, flags=None):
    return pltpu.CompilerParams(
        dimension_semantics=("arbitrary",) * n_grid_dims,
        vmem_limit_bytes=VMEM_LIMIT,
        flags=flags,
    )


def _rms_norm(x, g):
    ms = jnp.mean(jnp.square(x), axis=-1, keepdims=True)
    return x * lax.rsqrt(ms + NORM_EPS) * g


def _layer_norm(x, g, b):
    mu = jnp.mean(x, axis=-1, keepdims=True)
    xc = x - mu
    var = jnp.mean(jnp.square(xc), axis=-1, keepdims=True)
    return xc * lax.rsqrt(var + NORM_EPS) * g + b


def _rope_padded(r, cos, sin_lo, sin_hi):
    return (r * cos
            + pltpu.roll(r, LANES - QK_ROPE // 2, axis=1) * sin_lo
            + pltpu.roll(r, QK_ROPE // 2, axis=1) * sin_hi)


def _conv_gate_kernel(x_ref, state_ref, w_in_ref, conv_w_ref, w_co_ref,
                      ga_ref, sg_ref, conv_new_ref, u_scr, *, sample_rows):
    tm = x_ref.shape[0]
    xb = x_ref[...].astype(BF16)
    c = D_CONV
    u = _dot(xb, w_in_ref[:, c:2 * c]) * _dot(xb, w_in_ref[:, 2 * c:3 * c])
    w0 = conv_w_ref[0:1, :]
    w1 = conv_w_ref[1:2, :]
    w2 = conv_w_ref[2:3, :]
    if sample_rows is None:
        @pl.when(pl.program_id(1) == 0)
        def _():
            u_scr[0:6, :] = jnp.zeros((6, c), F32)
            u_scr[6:8, :] = state_ref[...]
        u_scr[8:8 + tm, :] = u
        v = w2 * u + w1 * u_scr[7:7 + tm, :] + w0 * u_scr[6:6 + tm, :]
        tail = u_scr[tm:tm + 8, :]
        u_scr[0:8, :] = tail
        conv_new_ref[...] = tail[6:8, :]
    else:
        s = sample_rows
        u_scr[0:2 * s, :] = state_ref[...]
        u_scr[2 * s:2 * s + tm, :] = u
        v = w2 * u + w1 * u_scr[s:s + tm, :] + w0 * u_scr[0:tm, :]
        conv_new_ref[...] = u_scr[tm:tm + 2 * s, :]
    a_in = (_dot(xb, w_in_ref[:, 0:c]) * v).astype(BF16)
    a = _dot(a_in, w_co_ref[...])
    ga_ref[...] = jax.nn.sigmoid(_dot(xb, w_in_ref[:, 3 * c:3 * c + D_MODEL])) * a
    sg_ref[...] = jax.nn.sigmoid(_dot(xb, w_in_ref[:, 3 * c + D_MODEL:3 * c + 2 * D_MODEL]))


def _conv_gate(x, state, w_in_cg, conv_w, w_co, *, sample_rows):
    g, t, d = x.shape
    tm = min(TOKEN_TILE, t)
    n_state = state.shape[1]
    scr_rows = 8 + tm if sample_rows is None else 2 * sample_rows + tm
    tok = lambda w: pl.BlockSpec((None, tm, w), lambda b, i: (b, i, 0))
    return pl.pallas_call(
        functools.partial(_conv_gate_kernel, sample_rows=sample_rows),
        grid=(g, t // tm),
        in_specs=[tok(d),
                  pl.BlockSpec((None, n_state, D_CONV), lambda b, i: (b, 0, 0)),
                  _const_spec(w_in_cg.shape), _const_spec(conv_w.shape), _const_spec(w_co.shape)],
        out_specs=[tok(D_MODEL), tok(D_MODEL),
                   pl.BlockSpec((None, n_state, D_CONV), lambda b, i: (b, 0, 0))],
        out_shape=[jax.ShapeDtypeStruct((g, t, D_MODEL), F32),
                   jax.ShapeDtypeStruct((g, t, D_MODEL), F32),
                   jax.ShapeDtypeStruct((g, n_state, D_CONV), F32)],
        scratch_shapes=[pltpu.VMEM((scr_rows, D_CONV), F32)],
        compiler_params=_params(2),
        name="conv_gate",
    )(x, state, w_in_cg, conv_w, w_co)


def _attn_front_kernel(x_ref, cos_ref, sin_lo_ref, sin_hi_ref, w_small_ref, qg_ref, kvg_ref,
                       w_qb_ref, w_kb_ref, w_vb_ref,
                       q_ref, k_ref, vt_ref, lat_ref, krope_ref):
    xb = x_ref[...].astype(BF16)
    small = _dot(xb, w_small_ref[...])
    cos = cos_ref[...]
    sin_lo = sin_lo_ref[...]
    sin_hi = sin_hi_ref[...]

    qn = _rms_norm(small[:, 0:Q_LORA], qg_ref[...]).astype(BF16)
    q = _dot(qn, w_qb_ref[...]) * QK_LOG2_SCALE
    for h in range(N_HEADS):
        lo = h * HEAD_PAD
        q_ref[:, lo:lo + LANES] = q[:, lo:lo + LANES].astype(BF16)
        q_ref[:, lo + LANES:lo + HEAD_PAD] = _rope_padded(
            q[:, lo + LANES:lo + HEAD_PAD], cos, sin_lo, sin_hi).astype(BF16)

    lat = _rms_norm(small[:, Q_LORA:Q_LORA + KV_LORA], kvg_ref[...])
    lat_ref[...] = lat
    latb = lat.astype(BF16)
    krope = _rope_padded(small[:, Q_LORA + KV_LORA:SMALL_W], cos, sin_lo, sin_hi)
    krope_ref[...] = krope[:, 0:QK_ROPE]
    kropeb = krope.astype(BF16)
    kn = _dot(latb, w_kb_ref[...]).astype(BF16)
    for h in range(N_HEADS):
        lo = h * HEAD_PAD
        k_ref[:, lo:lo + LANES] = kn[:, h * QK_NOPE:(h + 1) * QK_NOPE]
        k_ref[:, lo + LANES:lo + HEAD_PAD] = kropeb
    vt_ref[...] = _dot_nt(w_vb_ref[...], latb).astype(BF16)


def _attn_front(x, tables, w_small, qg, kvg, w_qb, w_kb, w_vb_t):
    g, t, d = x.shape
    tm = min(TOKEN_TILE, t)
    tok = lambda w: pl.BlockSpec((None, tm, w), lambda b, i: (b, i, 0))
    tab = pl.BlockSpec((tm, LANES), lambda b, i: (i, 0))
    return pl.pallas_call(
        _attn_front_kernel,
        grid=(g, t // tm),
        in_specs=[tok(d), tab, tab, tab,
                  _const_spec(w_small.shape), _const_spec(qg.shape), _const_spec(kvg.shape),
                  _const_spec(w_qb.shape), _const_spec(w_kb.shape), _const_spec(w_vb_t.shape)],
        out_specs=[tok(N_HEADS * HEAD_PAD), tok(N_HEADS * HEAD_PAD),
                   pl.BlockSpec((None, None, N_HEADS * V_DIM, tm), lambda b, i: (b, i, 0, 0)),
                   tok(KV_LORA), tok(QK_ROPE)],
        out_shape=[jax.ShapeDtypeStruct((g, t, N_HEADS * HEAD_PAD), BF16),
                   jax.ShapeDtypeStruct((g, t, N_HEADS * HEAD_PAD), BF16),
                   jax.ShapeDtypeStruct((g, t // tm, N_HEADS * V_DIM, tm), BF16),
                   jax.ShapeDtypeStruct((g, t, KV_LORA), F32),
                   jax.ShapeDtypeStruct((g, t, QK_ROPE), F32)],
        compiler_params=_params(2),
        name="attn_front",
    )(x, *tables, w_small, qg, kvg, w_qb, w_kb, w_vb_t)


def _softmax_step(s, v, m, l, acc):
    m_new = jnp.maximum(m, jnp.max(s, axis=-1, keepdims=True))
    alpha = jnp.exp2(m - m_new)
    p = jnp.exp2(s - m_new)
    l = alpha * l + jnp.sum(p, axis=-1, keepdims=True)
    acc = alpha * acc + _dot(p.astype(BF16), v)
    return m_new, l, acc


def _softmax_init(rows, width):
    return (jnp.full((rows, 1), MASK_VALUE, F32), jnp.zeros((rows, 1), F32), jnp.zeros((rows, width), F32))


def _prompt_attn_kernel(q_ref, k_ref, vt_ref, o_ref, m_scr, l_scr, acc_scr):
    tq = q_ref.shape[0]
    tk = vt_ref.shape[2]
    qi = pl.program_id(2)
    heads = range(q_ref.shape[1] // HEAD_PAD)
    qs = [q_ref[:, h * HEAD_PAD:(h + 1) * HEAD_PAD] for h in heads]
    n_full = (qi * tq) // tk

    def reduce_keys(x, op):
        part = op(x.reshape(REDUCE_FANIN, tk // REDUCE_FANIN, tq), axis=0)
        return op(part, axis=0, keepdims=True)

    def chunk(j, diagonal=False):
        start = pl.multiple_of(j * tk, tk)
        s = [_dot_nt(k_ref[pl.ds(start, tk), h * HEAD_PAD:(h + 1) * HEAD_PAD], qs[h]) for h in heads]
        for h in heads:
            s_t = s[h]
            if diagonal:
                key = j * tk + lax.broadcasted_iota(jnp.int32, (tk, tq), 0)
                qry = qi * tq + lax.broadcasted_iota(jnp.int32, (tk, tq), 1)
                s_t = jnp.where(key <= qry, s_t, MASK_VALUE)
            m = m_scr[h]
            m_new = jnp.maximum(m, reduce_keys(s_t, jnp.max))
            alpha = jnp.exp2(m - m_new)
            p_t = jnp.exp2(s_t - m_new)
            m_scr[h] = m_new
            l_scr[h] = alpha * l_scr[h] + reduce_keys(p_t, jnp.sum)
            acc_scr[h] = alpha * acc_scr[h] + _dot(vt_ref[j, h * V_DIM:(h + 1) * V_DIM, :], p_t.astype(BF16))

    m_scr[...] = jnp.full(m_scr.shape, MASK_VALUE, F32)
    l_scr[...] = jnp.zeros(l_scr.shape, F32)
    acc_scr[...] = jnp.zeros(acc_scr.shape, F32)

    @pl.loop(0, n_full)
    def _(j):
        chunk(j)

    chunk(n_full, diagonal=True)
    for h in heads:
        o_ref[:, h * V_DIM:(h + 1) * V_DIM] = jnp.transpose(acc_scr[h] / l_scr[h]).astype(o_ref.dtype)


def _prompt_attention(q, k, v_t):
    b, t, _ = q.shape
    tq = ATTN_Q_TILE
    g = ATTN_HEADS_PER_STEP
    n_k, _, tk = v_t.shape[1:]
    assert tk % tq == 0 and t == n_k * tk and N_HEADS % g == 0
    return pl.pallas_call(
        _prompt_attn_kernel,
        grid=(b, N_HEADS // g, t // tq),
        in_specs=[pl.BlockSpec((None, tq, g * HEAD_PAD), lambda bi, h, i: (bi, i, h)),
                  pl.BlockSpec((None, t, g * HEAD_PAD), lambda bi, h, i: (bi, 0, h),
                               pipeline_mode=pl.Buffered(1)),
                  pl.BlockSpec((None, n_k, g * V_DIM, tk), lambda bi, h, i: (bi, 0, h, 0),
                               pipeline_mode=pl.Buffered(1))],
        out_specs=pl.BlockSpec((None, tq, g * V_DIM), lambda bi, h, i: (bi, i, h)),
        out_shape=jax.ShapeDtypeStruct((b, t, N_HEADS * V_DIM), BF16),
        scratch_shapes=[pltpu.VMEM((g, 1, tq), F32), pltpu.VMEM((g, 1, tq), F32),
                        pltpu.VMEM((g, V_DIM, tq), F32)],
        compiler_params=_params(3),
        name="prompt_attention",
    )(q, k, v_t)


def _sample_attn_kernel(pt_ref, ql_ref, qr_ref, nl_ref, nr_ref, lat_hbm, ropet_hbm,
                        o_ref, lat_buf, rope_buf, sems, m_scr, l_scr, acc_scr):
    n = PAGES_PER_STEP
    b = pl.program_id(0)
    c = pl.program_id(1)
    n_chunks = pl.num_programs(1)
    step = b * n_chunks + c
    n_steps = pl.num_programs(0) * n_chunks
    slot = lax.rem(step, 2)
    rows = ql_ref.shape[0]

    def page_copies(bb, cc, sl):
        copies = []
        for j in range(n):
            page = pt_ref[bb, cc * n + j]
            keys = pl.ds(j * PAGE_SIZE, PAGE_SIZE)
            copies.append(pltpu.make_async_copy(lat_hbm.at[page], lat_buf.at[sl, keys, :], sems.at[0, sl]))
            copies.append(pltpu.make_async_copy(ropet_hbm.at[page], rope_buf.at[sl, :, keys], sems.at[1, sl]))
        return copies

    def start_all(copies):
        for i, cp in enumerate(copies):
            cp.start(priority=(i // 2) % 2)

    @pl.when(step == 0)
    def _():
        start_all(page_copies(b, c, slot))

    @pl.when(step + 1 < n_steps)
    def _():
        wrap = c + 1 == n_chunks
        start_all(page_copies(jnp.where(wrap, b + 1, b), jnp.where(wrap, 0, c + 1), 1 - slot))

    @pl.when(c == 0)
    def _():
        m_scr[...] = jnp.full(m_scr.shape, MASK_VALUE, F32)
        l_scr[...] = jnp.zeros(l_scr.shape, F32)
        acc_scr[...] = jnp.zeros(acc_scr.shape, F32)

    for cp in page_copies(b, c, slot):
        cp.wait()

    ql = ql_ref[...]
    qr = qr_ref[...]
    span = n * PAGE_SIZE // SAMPLE_CHAINS
    kls = [lat_buf[slot, h * span:(h + 1) * span, :].astype(BF16) for h in range(SAMPLE_CHAINS)]
    scores = [_dot_nt(ql, kls[h]) + _dot(qr, rope_buf[slot, :, h * span:(h + 1) * span].astype(BF16))
              for h in range(SAMPLE_CHAINS)]
    for h in range(SAMPLE_CHAINS):
        m, l, acc = _softmax_step(scores[h], kls[h], m_scr[h], l_scr[h], acc_scr[h])
        m_scr[h] = m
        l_scr[h] = l
        acc_scr[h] = acc

    @pl.when(c == n_chunks - 1)
    def _():
        m, l, acc = m_scr[0], l_scr[0], acc_scr[0]
        for h in range(1, SAMPLE_CHAINS):
            m_new = jnp.maximum(m, m_scr[h])
            wa = jnp.exp2(m - m_new)
            wb = jnp.exp2(m_scr[h] - m_new)
            l = wa * l + wb * l_scr[h]
            acc = wa * acc + wb * acc_scr[h]
            m = m_new
        nl = nl_ref[...].astype(BF16)
        s_new = _dot_nt(ql, nl) + _dot_nt(qr, nr_ref[...].astype(BF16))
        tok = lax.shift_right_logical(lax.broadcasted_iota(jnp.int32, (rows, NEW_KEY_PAD), 0),
                                      int(np.log2(N_HEADS)))
        key = lax.broadcasted_iota(jnp.int32, (rows, NEW_KEY_PAD), 1)
        s_new = jnp.where(key <= tok, s_new, MASK_VALUE)
        _, l2, acc2 = _softmax_step(s_new, nl, m, l, acc)
        o_ref[...] = (acc2 / l2).astype(o_ref.dtype)


def _sample_attention(page_table, q_lat, q_rope, new_lat, new_rope, cache_lat, cache_rope_t):
    s, rows, _ = q_lat.shape
    n_pages = page_table.shape[1]
    n = PAGES_PER_STEP
    assert n_pages % n == 0 and rows // N_HEADS <= NEW_KEY_PAD
    seq = lambda r, w: pl.BlockSpec((None, r, w), lambda b, c, pt: (b, 0, 0))
    hbm = pl.BlockSpec(memory_space=pl.ANY)
    grid_spec = pltpu.PrefetchScalarGridSpec(
        num_scalar_prefetch=1,
        grid=(s, n_pages // n),
        in_specs=[seq(rows, KV_LORA), seq(rows, QK_ROPE), seq(NEW_KEY_PAD, KV_LORA), seq(NEW_KEY_PAD, QK_ROPE),
                  hbm, hbm],
        out_specs=seq(rows, KV_LORA),
        scratch_shapes=[pltpu.VMEM((2, n * PAGE_SIZE, KV_LORA), F32),
                        pltpu.VMEM((2, QK_ROPE, n * PAGE_SIZE), F32),
                        pltpu.SemaphoreType.DMA((2, 2)),
                        pltpu.VMEM((SAMPLE_CHAINS, rows, 1), F32),
                        pltpu.VMEM((SAMPLE_CHAINS, rows, 1), F32),
                        pltpu.VMEM((SAMPLE_CHAINS, rows, KV_LORA), F32)],
    )
    return pl.pallas_call(
        _sample_attn_kernel,
        grid_spec=grid_spec,
        out_shape=jax.ShapeDtypeStruct((s, rows, KV_LORA), BF16),
        compiler_params=_params(2),
        name="sample_attention",
    )(page_table, q_lat, q_rope, new_lat, new_rope, cache_lat, cache_rope_t)


def _head_matmul_kernel(x_ref, w_ref, o_ref):
    o_ref[...] = _dot(x_ref[...], w_ref[...]).astype(o_ref.dtype)


def _head_matmul(x, w, *, in_block_stride):
    m = x.shape[0]
    h, xw, yw = w.shape
    return pl.pallas_call(
        _head_matmul_kernel,
        grid=(h,),
        in_specs=[pl.BlockSpec((m, xw), lambda i: (0, i * in_block_stride)),
                  pl.BlockSpec((None, xw, yw), lambda i: (i, 0, 0))],
        out_specs=pl.BlockSpec((m, yw), lambda i: (0, i)),
        out_shape=jax.ShapeDtypeStruct((m, h * yw), BF16),
        compiler_params=_params(1),
        name="head_matmul",
    )(x, w)


def _merge_ln_kernel(o_ref, ga_ref, sg_ref, x_ref, w_ao_ref, w_mix_ref, g_ref, b_ref, h1_ref, *, alpha):
    m = _dot(o_ref[...], w_ao_ref[...])
    mix = (ga_ref[...] + sg_ref[...] * m).astype(BF16)
    z = _dot(mix, w_mix_ref[...])
    h1_ref[...] = _layer_norm(alpha * x_ref[...] + z, g_ref[...], b_ref[...])


def _merge_ln(o, ga, sg, x, w_ao, w_mix, g, b, *, alpha):
    t = x.shape[0]
    tm = min(TOKEN_TILE, t)
    tok = lambda w: pl.BlockSpec((tm, w), lambda i: (i, 0))
    return pl.pallas_call(
        functools.partial(_merge_ln_kernel, alpha=alpha),
        grid=(t // tm,),
        in_specs=[tok(o.shape[1]), tok(D_MODEL), tok(D_MODEL), tok(D_MODEL),
                  _const_spec(w_ao.shape), _const_spec(w_mix.shape),
                  _const_spec(g.shape), _const_spec(b.shape)],
        out_specs=tok(D_MODEL),
        out_shape=jax.ShapeDtypeStruct((t, D_MODEL), F32),
        compiler_params=_params(1),
        name="merge_ln",
    )(o, ga, sg, x, w_ao, w_mix, g, b)


FF_CHUNK = 1024


def _ffn_ln_kernel(h1_ref, w1_ref, w2_ref, g_ref, b_ref, y_ref, *, alpha):
    h1 = h1_ref[...]
    hb = h1.astype(BF16)
    f = jnp.zeros(h1.shape, F32)
    for c in range(0, D_FF, FF_CHUNK):
        hid = jnp.square(jax.nn.relu(_dot(hb, w1_ref[:, c:c + FF_CHUNK]))).astype(BF16)
        f = f + _dot(hid, w2_ref[c:c + FF_CHUNK, :])
    y_ref[...] = _layer_norm(alpha * h1 + f, g_ref[...], b_ref[...])


def _ffn_ln(h1, w1, w2, g, b, *, alpha):
    t = h1.shape[0]
    tm = min(TOKEN_TILE, t)
    tok = pl.BlockSpec((tm, D_MODEL), lambda i: (i, 0))
    return pl.pallas_call(
        functools.partial(_ffn_ln_kernel, alpha=alpha),
        grid=(t // tm,),
        in_specs=[tok, _const_spec(w1.shape), _const_spec(w2.shape),
                  _const_spec(g.shape), _const_spec(b.shape)],
        out_specs=tok,
        out_shape=jax.ShapeDtypeStruct((t, D_MODEL), F32),
        compiler_params=_params(1),
        name="ffn_ln",
    )(h1, w1, w2, g, b)


def _rope_tables(pos):
    freqs = ROPE_THETA ** (-jnp.arange(0, QK_ROPE, 2, dtype=F32) / QK_ROPE)
    ang = pos.astype(F32)[:, None] * freqs[None, :]
    c, s = jnp.cos(ang), jnp.sin(ang)
    z = jnp.zeros_like(c)
    return (jnp.concatenate([c, c, z, z], axis=1),
            jnp.concatenate([-s, z, z, z], axis=1),
            jnp.concatenate([z, s, z, z], axis=1))


def _layer_weights(w_in, conv_w, q_norm_g, w_qb, kv_norm_g, w_kb, w_vb, w_conv_out, w_attn_out,
                   w_mix_out, ln1_g, ln1_b, w_ff1, w_ff2, ln2_g, ln2_b):
    c = D_CONV
    o_q = 3 * c
    o_kv = o_q + Q_LORA
    o_kr = o_kv + KV_LORA
    o_gc = o_kr + QK_ROPE
    o_ga = o_gc + D_MODEL
    d = w_in.shape[0]
    w_in_cg = jnp.concatenate([w_in[:, 0:o_q], w_in[:, o_gc:o_ga + D_MODEL]], axis=1).astype(BF16)
    w_small = jnp.concatenate(
        [w_in[:, o_q:o_gc], jnp.zeros((d, LANES - QK_ROPE), w_in.dtype)], axis=1).astype(BF16)
    wq = w_qb.reshape(Q_LORA, N_HEADS, QK_NOPE + QK_ROPE)
    wq = jnp.concatenate(
        [wq, jnp.zeros((Q_LORA, N_HEADS, HEAD_PAD - QK_NOPE - QK_ROPE), w_qb.dtype)], axis=2)
    return dict(
        w_in_cg=w_in_cg, w_small=w_small, conv_w=conv_w,
        qg=q_norm_g[None, :], kvg=kv_norm_g[None, :],
        w_qb=wq.reshape(Q_LORA, N_HEADS * HEAD_PAD).astype(BF16),
        w_kb=w_kb.reshape(KV_LORA, N_HEADS * QK_NOPE).astype(BF16),
        w_vb_t=w_vb.reshape(KV_LORA, N_HEADS * V_DIM).T.astype(BF16),
        w_kb_heads=jnp.transpose(w_kb, (1, 2, 0)).astype(BF16),
        w_vb_heads=jnp.transpose(w_vb, (1, 0, 2)).astype(BF16),
        w_co=w_conv_out.astype(BF16), w_ao=w_attn_out.astype(BF16), w_mix=w_mix_out.astype(BF16),
        ln1_g=ln1_g[None, :], ln1_b=ln1_b[None, :],
        w_ff1=w_ff1.astype(BF16), w_ff2=w_ff2.astype(BF16),
        ln2_g=ln2_g[None, :], ln2_b=ln2_b[None, :],
    )


def _tail(o, ga, sg, x, lw, alpha):
    h1 = _merge_ln(o, ga, sg, x, lw["w_ao"], lw["w_mix"], lw["ln1_g"], lw["ln1_b"], alpha=alpha)
    return _ffn_ln(h1, lw["w_ff1"], lw["w_ff2"], lw["ln2_g"], lw["ln2_b"], alpha=alpha)


def _prompt_layer(x, conv_prev, lw, alpha):
    b, t, d = x.shape
    ga, sg, conv_new = _conv_gate(x, conv_prev, lw["w_in_cg"], lw["conv_w"], lw["w_co"], sample_rows=None)
    q, k, v_t, lat, krope = _attn_front(x, _rope_tables(jnp.arange(t)), lw["w_small"], lw["qg"], lw["kvg"],
                                        lw["w_qb"], lw["w_kb"], lw["w_vb_t"])
    o = _prompt_attention(q, k, v_t)
    y = _tail(o.reshape(b * t, -1), ga.reshape(b * t, d), sg.reshape(b * t, d), x.reshape(b * t, d), lw, alpha)
    return y.reshape(b, t, d), lat, krope, conv_new


def _sample_layer(x, conv_prev, cache_lat, cache_rope, page_table, lw, alpha):
    s, t, d = x.shape
    past = page_table.shape[1] * PAGE_SIZE
    xt = jnp.transpose(x, (1, 0, 2)).reshape(1, t * s, d)
    state = jnp.transpose(conv_prev, (1, 0, 2)).reshape(1, (CONV_W - 1) * s, D_CONV)
    ga, sg, conv_new = _conv_gate(xt, state, lw["w_in_cg"], lw["conv_w"], lw["w_co"], sample_rows=s)
    tables = _rope_tables(jnp.repeat(past + jnp.arange(t), s))
    q, _, _, lat, krope = _attn_front(xt, tables, lw["w_small"], lw["qg"], lw["kvg"],
                                      lw["w_qb"], lw["w_kb"], lw["w_vb_t"])
    q = q[0]
    q_lat = _head_matmul(q, lw["w_kb_heads"], in_block_stride=HEAD_PAD // QK_NOPE)

    def per_seq(a, w):
        return jnp.transpose(a.reshape(t, s, N_HEADS, w), (1, 0, 2, 3)).reshape(s, t * N_HEADS, w)

    q_rope = per_seq(q.reshape(t * s, N_HEADS, HEAD_PAD)[:, :, QK_NOPE:QK_NOPE + QK_ROPE].reshape(t * s, -1), QK_ROPE)
    lat_seq = jnp.transpose(lat.reshape(t, s, KV_LORA), (1, 0, 2))
    krope_seq = jnp.transpose(krope.reshape(t, s, QK_ROPE), (1, 0, 2))
    pad = ((0, 0), (0, NEW_KEY_PAD - t), (0, 0))
    o_lat = _sample_attention(page_table, per_seq(q_lat, KV_LORA), q_rope,
                              jnp.pad(lat_seq, pad), jnp.pad(krope_seq, pad),
                              cache_lat, jnp.swapaxes(cache_rope, 1, 2))
    o_lat = jnp.transpose(o_lat.reshape(s, t, N_HEADS, KV_LORA), (1, 0, 2, 3)).reshape(t * s, -1)
    o = _head_matmul(o_lat, lw["w_vb_heads"], in_block_stride=1)
    y = _tail(o, ga[0], sg[0], xt[0], lw, alpha)
    y = jnp.transpose(y.reshape(t, s, d), (1, 0, 2))
    conv_new = jnp.transpose(conv_new.reshape(CONV_W - 1, s, D_CONV), (1, 0, 2))
    return y, lat_seq, krope_seq, conv_new


def kernel(x_prompt, x_sample, cache_latent, cache_krope, state_conv, page_table, w_in, conv_w, q_norm_g, w_qb, kv_norm_g, w_kb, w_vb, w_conv_out, w_attn_out, w_mix_out, ln1_g, ln1_b, w_ff1, w_ff2, ln2_g, ln2_b):
    depth = w_in.shape[0]
    alpha = (2 * depth) ** 0.25
    conv_zero = jnp.zeros((x_prompt.shape[0], CONV_W - 1, D_CONV), x_prompt.dtype)
    y_p, y_s = x_prompt, x_sample
    outs = [[] for _ in range(6)]
    for l in range(depth):
        lw = _layer_weights(w_in[l], conv_w[l], q_norm_g[l], w_qb[l], kv_norm_g[l], w_kb[l], w_vb[l],
                            w_conv_out[l], w_attn_out[l], w_mix_out[l], ln1_g[l], ln1_b[l],
                            w_ff1[l], w_ff2[l], ln2_g[l], ln2_b[l])
        y_p, lp, rp, cp = _prompt_layer(y_p, conv_zero, lw, alpha)
        y_s, ls, rs, cs = _sample_layer(y_s, state_conv[l], cache_latent[l], cache_krope[l], page_table, lw, alpha)
        for acc, val in zip(outs, (lp, rp, cp, ls, rs, cs)):
            acc.append(val)
    return (y_p, y_s) + tuple(jnp.stack(o) for o in outs)
```

```python
import functools

import jax
import jax.numpy as jnp
import numpy as np
from jax import lax
from jax.experimental import pallas as pl
from jax.experimental.pallas import tpu as pltpu

F32 = jnp.float32
BF16 = jnp.bfloat16

D_MODEL = 1024
D_CONV = 1024
CONV_W = 3
N_HEADS = 16
QK_NOPE = 128
QK_ROPE = 64
V_DIM = 128
Q_LORA = 256
KV_LORA = 256
ROPE_THETA = 10000.0
D_FF = 4 * D_MODEL
NORM_EPS = 1e-5
PAGE_SIZE = 128
MASK_VALUE = -1e30
ATTN_SCALE = (QK_NOPE + QK_ROPE) ** -0.5

LANES = 128
HEAD_PAD = 2 * LANES
SMALL_W = Q_LORA + KV_LORA + LANES
VMEM_LIMIT = 56 * 1024 * 1024

QK_LOG2_SCALE = ATTN_SCALE * float(np.log2(np.e))

TOKEN_TILE = 512
ATTN_Q_TILE = 512
ATTN_HEADS_PER_STEP = 8
REDUCE_FANIN = 8
PAGES_PER_STEP = 64
SAMPLE_CHAINS = 8
NEW_KEY_PAD = 16


def _dot(a, b):
    return jnp.dot(a, b, preferred_element_type=F32)


def _dot_nt(a, b):
    return lax.dot_general(a, b, (((1,), (1,)), ((), ())), preferred_element_type=F32)


def _const_spec(shape):
    zeros = (0,) * len(shape)
    return pl.BlockSpec(shape, lambda *_: zeros, pipeline_mode=pl.Buffered(1))


def _params(n_grid_dims, flags=None):
    return pltpu.CompilerParams(
        dimension_semantics=("arbitrary",) * n_grid_dims,
        vmem_limit_bytes=VMEM_LIMIT,
        flags=flags,
    )


def _rms_norm(x, g):
    ms = jnp.mean(jnp.square(x), axis=-1, keepdims=True)
    return x * lax.rsqrt(ms + NORM_EPS) * g


def _layer_norm(x, g, b):
    mu = jnp.mean(x, axis=-1, keepdims=True)
    xc = x - mu
    var = jnp.mean(jnp.square(xc), axis=-1, keepdims=True)
    return xc * lax.rsqrt(var + NORM_EPS) * g + b


def _rope_padded(r, cos, sin_lo, sin_hi):
    return (r * cos
            + pltpu.roll(r, LANES - QK_ROPE // 2, axis=1) * sin_lo
            + pltpu.roll(r, QK_ROPE // 2, axis=1) * sin_hi)


def _conv_gate_kernel(x_ref, state_ref, w_in_ref, conv_w_ref, w_co_ref,
                      ga_ref, sg_ref, conv_new_ref, u_scr, *, sample_rows):
    tm = x_ref.shape[0]
    xb = x_ref[...].astype(BF16)
    c = D_CONV
    u = _dot(xb, w_in_ref[:, c:2 * c]) * _dot(xb, w_in_ref[:, 2 * c:3 * c])
    w0 = conv_w_ref[0:1, :]
    w1 = conv_w_ref[1:2, :]
    w2 = conv_w_ref[2:3, :]
    if sample_rows is None:
        @pl.when(pl.program_id(1) == 0)
        def _():
            u_scr[0:6, :] = jnp.zeros((6, c), F32)
            u_scr[6:8, :] = state_ref[...]
        u_scr[8:8 + tm, :] = u
        v = w2 * u + w1 * u_scr[7:7 + tm, :] + w0 * u_scr[6:6 + tm, :]
        tail = u_scr[tm:tm + 8, :]
        u_scr[0:8, :] = tail
        conv_new_ref[...] = tail[6:8, :]
    else:
        s = sample_rows
        u_scr[0:2 * s, :] = state_ref[...]
        u_scr[2 * s:2 * s + tm, :] = u
        v = w2 * u + w1 * u_scr[s:s + tm, :] + w0 * u_scr[0:tm, :]
        conv_new_ref[...] = u_scr[tm:tm + 2 * s, :]
    a_in = (_dot(xb, w_in_ref[:, 0:c]) * v).astype(BF16)
    a = _dot(a_in, w_co_ref[...])
    ga_ref[...] = jax.nn.sigmoid(_dot(xb, w_in_ref[:, 3 * c:3 * c + D_MODEL])) * a
    sg_ref[...] = jax.nn.sigmoid(_dot(xb, w_in_ref[:, 3 * c + D_MODEL:3 * c + 2 * D_MODEL]))


def _conv_gate(x, state, w_in_cg, conv_w, w_co, *, sample_rows):
    g, t, d = x.shape
    tm = min(TOKEN_TILE, t)
    n_state = state.shape[1]
    scr_rows = 8 + tm if sample_rows is None else 2 * sample_rows + tm
    tok = lambda w: pl.BlockSpec((None, tm, w), lambda b, i: (b, i, 0))
    return pl.pallas_call(
        functools.partial(_conv_gate_kernel, sample_rows=sample_rows),
        grid=(g, t // tm),
        in_specs=[tok(d),
                  pl.BlockSpec((None, n_state, D_CONV), lambda b, i: (b, 0, 0)),
                  _const_spec(w_in_cg.shape), _const_spec(conv_w.shape), _const_spec(w_co.shape)],
        out_specs=[tok(D_MODEL), tok(D_MODEL),
                   pl.BlockSpec((None, n_state, D_CONV), lambda b, i: (b, 0, 0))],
        out_shape=[jax.ShapeDtypeStruct((g, t, D_MODEL), F32),
                   jax.ShapeDtypeStruct((g, t, D_MODEL), F32),
                   jax.ShapeDtypeStruct((g, n_state, D_CONV), F32)],
        scratch_shapes=[pltpu.VMEM((scr_rows, D_CONV), F32)],
        compiler_params=_params(2),
        name="conv_gate",
    )(x, state, w_in_cg, conv_w, w_co)


def _attn_front_kernel(x_ref, cos_ref, sin_lo_ref, sin_hi_ref, w_small_ref, qg_ref, kvg_ref,
                       w_qb_ref, w_kb_ref, w_vb_ref,
                       q_ref, k_ref, vt_ref, lat_ref, krope_ref):
    xb = x_ref[...].astype(BF16)
    small = _dot(xb, w_small_ref[...])
    cos = cos_ref[...]
    sin_lo = sin_lo_ref[...]
    sin_hi = sin_hi_ref[...]

    qn = _rms_norm(small[:, 0:Q_LORA], qg_ref[...]).astype(BF16)
    q = _dot(qn, w_qb_ref[...]) * QK_LOG2_SCALE
    for h in range(N_HEADS):
        lo = h * HEAD_PAD
        q_ref[:, lo:lo + LANES] = q[:, lo:lo + LANES].astype(BF16)
        q_ref[:, lo + LANES:lo + HEAD_PAD] = _rope_padded(
            q[:, lo + LANES:lo + HEAD_PAD], cos, sin_lo, sin_hi).astype(BF16)

    lat = _rms_norm(small[:, Q_LORA:Q_LORA + KV_LORA], kvg_ref[...])
    lat_ref[...] = lat
    latb = lat.astype(BF16)
    krope = _rope_padded(small[:, Q_LORA + KV_LORA:SMALL_W], cos, sin_lo, sin_hi)
    krope_ref[...] = krope[:, 0:QK_ROPE]
    kropeb = krope.astype(BF16)
    kn = _dot(latb, w_kb_ref[...]).astype(BF16)
    for h in range(N_HEADS):
        lo = h * HEAD_PAD
        k_ref[:, lo:lo + LANES] = kn[:, h * QK_NOPE:(h + 1) * QK_NOPE]
        k_ref[:, lo + LANES:lo + HEAD_PAD] = kropeb
    vt_ref[...] = _dot_nt(w_vb_ref[...], latb).astype(BF16)


def _attn_front(x, tables, w_small, qg, kvg, w_qb, w_kb, w_vb_t):
    g, t, d = x.shape
    tm = min(TOKEN_TILE, t)
    tok = lambda w: pl.BlockSpec((None, tm, w), lambda b, i: (b, i, 0))
    tab = pl.BlockSpec((tm, LANES), lambda b, i: (i, 0))
    return pl.pallas_call(
        _attn_front_kernel,
        grid=(g, t // tm),
        in_specs=[tok(d), tab, tab, tab,
                  _const_spec(w_small.shape), _const_spec(qg.shape), _const_spec(kvg.shape),
                  _const_spec(w_qb.shape), _const_spec(w_kb.shape), _const_spec(w_vb_t.shape)],
        out_specs=[tok(N_HEADS * HEAD_PAD), tok(N_HEADS * HEAD_PAD),
                   pl.BlockSpec((None, None, N_HEADS * V_DIM, tm), lambda b, i: (b, i, 0, 0)),
                   tok(KV_LORA), tok(QK_ROPE)],
        out_shape=[jax.ShapeDtypeStruct((g, t, N_HEADS * HEAD_PAD), BF16),
                   jax.ShapeDtypeStruct((g, t, N_HEADS * HEAD_PAD), BF16),
                   jax.ShapeDtypeStruct((g, t // tm, N_HEADS * V_DIM, tm), BF16),
                   jax.ShapeDtypeStruct((g, t, KV_LORA), F32),
                   jax.ShapeDtypeStruct((g, t, QK_ROPE), F32)],
        compiler_params=_params(2),
        name="attn_front",
    )(x, *tables, w_small, qg, kvg, w_qb, w_kb, w_vb_t)


def _softmax_step(s, v, m, l, acc):
    m_new = jnp.maximum(m, jnp.max(s, axis=-1, keepdims=True))
    alpha = jnp.exp2(m - m_new)
    p = jnp.exp2(s - m_new)
    l = alpha * l + jnp.sum(p, axis=-1, keepdims=True)
    acc = alpha * acc + _dot(p.astype(BF16), v)
    return m_new, l, acc


def _softmax_init(rows, width):
    return (jnp.full((rows, 1), MASK_VALUE, F32), jnp.zeros((rows, 1), F32), jnp.zeros((rows, width), F32))


def _prompt_attn_kernel(q_ref, k_ref, vt_ref, o_ref, m_scr, l_scr, acc_scr):
    tq = q_ref.shape[0]
    tk = vt_ref.shape[2]
    qi = pl.program_id(2)
    heads = range(q_ref.shape[1] // HEAD_PAD)

    def reduce_keys(x, op):
        part = op(x.reshape(REDUCE_FANIN, x.shape[0] // REDUCE_FANIN, x.shape[1]), axis=0)
        return op(part, axis=0, keepdims=True)

    def block(j, k_lo=0, k_n=tk, q_lo=0, q_n=tq, diagonal=False):
        start = pl.multiple_of(j * tk, tk) + k_lo
        qcols = slice(q_lo, q_lo + q_n)
        s = [_dot_nt(k_ref[pl.ds(start, k_n), h * HEAD_PAD:(h + 1) * HEAD_PAD],
                     q_ref[qcols, h * HEAD_PAD:(h + 1) * HEAD_PAD]) for h in heads]
        for h in heads:
            s_t = s[h]
            if diagonal:
                key = k_lo + lax.broadcasted_iota(jnp.int32, (k_n, q_n), 0)
                qry = q_lo + lax.broadcasted_iota(jnp.int32, (k_n, q_n), 1)
                s_t = jnp.where(key <= qry, s_t, MASK_VALUE)
            m = m_scr[h, :, qcols]
            m_new = jnp.maximum(m, reduce_keys(s_t, jnp.max))
            alpha = jnp.exp2(m - m_new)
            p_t = jnp.exp2(s_t - m_new)
            m_scr[h, :, qcols] = m_new
            l_scr[h, :, qcols] = alpha * l_scr[h, :, qcols] + reduce_keys(p_t, jnp.sum)
            acc_scr[h, :, qcols] = alpha * acc_scr[h, :, qcols] + _dot(
                vt_ref[j, h * V_DIM:(h + 1) * V_DIM, k_lo:k_lo + k_n], p_t.astype(BF16))

    m_scr[...] = jnp.full(m_scr.shape, MASK_VALUE, F32)
    l_scr[...] = jnp.zeros(l_scr.shape, F32)
    acc_scr[...] = jnp.zeros(acc_scr.shape, F32)

    @pl.loop(0, qi)
    def _(j):
        block(j)

    half = tk // 2
    block(qi, 0, half, 0, tq, diagonal=True)
    block(qi, half, half, half, tq - half, diagonal=True)
    for h in heads:
        o_ref[:, h * V_DIM:(h + 1) * V_DIM] = jnp.transpose(acc_scr[h] / l_scr[h]).astype(o_ref.dtype)


def _prompt_attention(q, k, v_t):
    b, t, _ = q.shape
    tq = ATTN_Q_TILE
    g = ATTN_HEADS_PER_STEP
    n_k, _, tk = v_t.shape[1:]
    assert tk == tq and t == n_k * tk and N_HEADS % g == 0
    return pl.pallas_call(
        _prompt_attn_kernel,
        grid=(b, N_HEADS // g, t // tq),
        in_specs=[pl.BlockSpec((None, tq, g * HEAD_PAD), lambda bi, h, i: (bi, i, h)),
                  pl.BlockSpec((None, t, g * HEAD_PAD), lambda bi, h, i: (bi, 0, h),
                               pipeline_mode=pl.Buffered(1)),
                  pl.BlockSpec((None, n_k, g * V_DIM, tk), lambda bi, h, i: (bi, 0, h, 0),
                               pipeline_mode=pl.Buffered(1))],
        out_specs=pl.BlockSpec((None, tq, g * V_DIM), lambda bi, h, i: (bi, i, h)),
        out_shape=jax.ShapeDtypeStruct((b, t, N_HEADS * V_DIM), BF16),
        scratch_shapes=[pltpu.VMEM((g, 1, tq), F32), pltpu.VMEM((g, 1, tq), F32),
                        pltpu.VMEM((g, V_DIM, tq), F32)],
        compiler_params=_params(3),
        name="prompt_attention",
    )(q, k, v_t)


def _sample_attn_kernel(pt_ref, ql_ref, qr_ref, nl_ref, nr_ref, lat_hbm, ropet_hbm,
                        o_ref, lat_buf, rope_buf, sems, m_scr, l_scr, acc_scr):
    n = PAGES_PER_STEP
    b = pl.program_id(0)
    c = pl.program_id(1)
    n_chunks = pl.num_programs(1)
    step = b * n_chunks + c
    n_steps = pl.num_programs(0) * n_chunks
    slot = lax.rem(step, 2)
    rows = ql_ref.shape[0]

    def page_copies(bb, cc, sl):
        copies = []
        for j in range(n):
            page = pt_ref[bb, cc * n + j]
            keys = pl.ds(j * PAGE_SIZE, PAGE_SIZE)
            copies.append(pltpu.make_async_copy(lat_hbm.at[page], lat_buf.at[sl, keys, :], sems.at[0, sl]))
            copies.append(pltpu.make_async_copy(ropet_hbm.at[page], rope_buf.at[sl, :, keys], sems.at[1, sl]))
        return copies

    def start_all(copies):
        for i, cp in enumerate(copies):
            cp.start(priority=(i // 2) % 2)

    @pl.when(step == 0)
    def _():
        start_all(page_copies(b, c, slot))

    @pl.when(step + 1 < n_steps)
    def _():
        wrap = c + 1 == n_chunks
        start_all(page_copies(jnp.where(wrap, b + 1, b), jnp.where(wrap, 0, c + 1), 1 - slot))

    @pl.when(c == 0)
    def _():
        m_scr[...] = jnp.full(m_scr.shape, MASK_VALUE, F32)
        l_scr[...] = jnp.zeros(l_scr.shape, F32)
        acc_scr[...] = jnp.zeros(acc_scr.shape, F32)

    for cp in page_copies(b, c, slot):
        cp.wait()

    ql = ql_ref[...]
    qr = qr_ref[...]
    span = n * PAGE_SIZE // SAMPLE_CHAINS
    kls = [lat_buf[slot, h * span:(h + 1) * span, :].astype(BF16) for h in range(SAMPLE_CHAINS)]
    scores = [_dot_nt(ql, kls[h]) + _dot(qr, rope_buf[slot, :, h * span:(h + 1) * span].astype(BF16))
              for h in range(SAMPLE_CHAINS)]
    for h in range(SAMPLE_CHAINS):
        m, l, acc = _softmax_step(scores[h], kls[h], m_scr[h], l_scr[h], acc_scr[h])
        m_scr[h] = m
        l_scr[h] = l
        acc_scr[h] = acc

    @pl.when(c == n_chunks - 1)
    def _():
        m, l, acc = m_scr[0], l_scr[0], acc_scr[0]
        for h in range(1, SAMPLE_CHAINS):
            m_new = jnp.maximum(m, m_scr[h])
            wa = jnp.exp2(m - m_new)
            wb = jnp.exp2(m_scr[h] - m_new)
            l = wa * l + wb * l_scr[h]
            acc = wa * acc + wb * acc_scr[h]
            m = m_new
        nl = nl_ref[...].astype(BF16)
        s_new = _dot_nt(ql, nl) + _dot_nt(qr, nr_ref[...].astype(BF16))
        tok = lax.shift_right_logical(lax.broadcasted_iota(jnp.int32, (rows, NEW_KEY_PAD), 0),
                                      int(np.log2(N_HEADS)))
        key = lax.broadcasted_iota(jnp.int32, (rows, NEW_KEY_PAD), 1)
        s_new = jnp.where(key <= tok, s_new, MASK_VALUE)
        _, l2, acc2 = _softmax_step(s_new, nl, m, l, acc)
        o_ref[...] = (acc2 / l2).astype(o_ref.dtype)


def _sample_attention(page_table, q_lat, q_rope, new_lat, new_rope, cache_lat, cache_rope_t):
    s, rows, _ = q_lat.shape
    n_pages = page_table.shape[1]
    n = PAGES_PER_STEP
    assert n_pages % n == 0 and rows // N_HEADS <= NEW_KEY_PAD
    seq = lambda r, w: pl.BlockSpec((None, r, w), lambda b, c, pt: (b, 0, 0))
    hbm = pl.BlockSpec(memory_space=pl.ANY)
    grid_spec = pltpu.PrefetchScalarGridSpec(
        num_scalar_prefetch=1,
        grid=(s, n_pages // n),
        in_specs=[seq(rows, KV_LORA), seq(rows, QK_ROPE), seq(NEW_KEY_PAD, KV_LORA), seq(NEW_KEY_PAD, QK_ROPE),
                  hbm, hbm],
        out_specs=seq(rows, KV_LORA),
        scratch_shapes=[pltpu.VMEM((2, n * PAGE_SIZE, KV_LORA), F32),
                        pltpu.VMEM((2, QK_ROPE, n * PAGE_SIZE), F32),
                        pltpu.SemaphoreType.DMA((2, 2)),
                        pltpu.VMEM((SAMPLE_CHAINS, rows, 1), F32),
                        pltpu.VMEM((SAMPLE_CHAINS, rows, 1), F32),
                        pltpu.VMEM((SAMPLE_CHAINS, rows, KV_LORA), F32)],
    )
    return pl.pallas_call(
        _sample_attn_kernel,
        grid_spec=grid_spec,
        out_shape=jax.ShapeDtypeStruct((s, rows, KV_LORA), BF16),
        compiler_params=_params(2),
        name="sample_attention",
    )(page_table, q_lat, q_rope, new_lat, new_rope, cache_lat, cache_rope_t)


def _head_matmul_kernel(x_ref, w_ref, o_ref):
    o_ref[...] = _dot(x_ref[...], w_ref[...]).astype(o_ref.dtype)


def _head_matmul(x, w, *, in_block_stride):
    m = x.shape[0]
    h, xw, yw = w.shape
    return pl.pallas_call(
        _head_matmul_kernel,
        grid=(h,),
        in_specs=[pl.BlockSpec((m, xw), lambda i: (0, i * in_block_stride)),
                  pl.BlockSpec((None, xw, yw), lambda i: (i, 0, 0))],
        out_specs=pl.BlockSpec((m, yw), lambda i: (0, i)),
        out_shape=jax.ShapeDtypeStruct((m, h * yw), BF16),
        compiler_params=_params(1),
        name="head_matmul",
    )(x, w)


def _merge_ln_kernel(o_ref, ga_ref, sg_ref, x_ref, w_ao_ref, w_mix_ref, g_ref, b_ref, h1_ref, *, alpha):
    m = _dot(o_ref[...], w_ao_ref[...])
    mix = (ga_ref[...] + sg_ref[...] * m).astype(BF16)
    z = _dot(mix, w_mix_ref[...])
    h1_ref[...] = _layer_norm(alpha * x_ref[...] + z, g_ref[...], b_ref[...])


def _merge_ln(o, ga, sg, x, w_ao, w_mix, g, b, *, alpha):
    t = x.shape[0]
    tm = min(TOKEN_TILE, t)
    tok = lambda w: pl.BlockSpec((tm, w), lambda i: (i, 0))
    return pl.pallas_call(
        functools.partial(_merge_ln_kernel, alpha=alpha),
        grid=(t // tm,),
        in_specs=[tok(o.shape[1]), tok(D_MODEL), tok(D_MODEL), tok(D_MODEL),
                  _const_spec(w_ao.shape), _const_spec(w_mix.shape),
                  _const_spec(g.shape), _const_spec(b.shape)],
        out_specs=tok(D_MODEL),
        out_shape=jax.ShapeDtypeStruct((t, D_MODEL), F32),
        compiler_params=_params(1),
        name="merge_ln",
    )(o, ga, sg, x, w_ao, w_mix, g, b)


FF_CHUNK = 1024


def _ffn_ln_kernel(h1_ref, w1_ref, w2_ref, g_ref, b_ref, y_ref, *, alpha):
    h1 = h1_ref[...]
    hb = h1.astype(BF16)
    f = jnp.zeros(h1.shape, F32)
    for c in range(0, D_FF, FF_CHUNK):
        hid = jnp.square(jax.nn.relu(_dot(hb, w1_ref[:, c:c + FF_CHUNK]))).astype(BF16)
        f = f + _dot(hid, w2_ref[c:c + FF_CHUNK, :])
    y_ref[...] = _layer_norm(alpha * h1 + f, g_ref[...], b_ref[...])


def _ffn_ln(h1, w1, w2, g, b, *, alpha):
    t = h1.shape[0]
    tm = min(TOKEN_TILE, t)
    tok = pl.BlockSpec((tm, D_MODEL), lambda i: (i, 0))
    return pl.pallas_call(
        functools.partial(_ffn_ln_kernel, alpha=alpha),
        grid=(t // tm,),
        in_specs=[tok, _const_spec(w1.shape), _const_spec(w2.shape),
                  _const_spec(g.shape), _const_spec(b.shape)],
        out_specs=tok,
        out_shape=jax.ShapeDtypeStruct((t, D_MODEL), F32),
        compiler_params=_params(1),
        name="ffn_ln",
    )(h1, w1, w2, g, b)


def _rope_tables(pos):
    freqs = ROPE_THETA ** (-jnp.arange(0, QK_ROPE, 2, dtype=F32) / QK_ROPE)
    ang = pos.astype(F32)[:, None] * freqs[None, :]
    c, s = jnp.cos(ang), jnp.sin(ang)
    z = jnp.zeros_like(c)
    return (jnp.concatenate([c, c, z, z], axis=1),
            jnp.concatenate([-s, z, z, z], axis=1),
            jnp.concatenate([z, s, z, z], axis=1))


def _layer_weights(w_in, conv_w, q_norm_g, w_qb, kv_norm_g, w_kb, w_vb, w_conv_out, w_attn_out,
                   w_mix_out, ln1_g, ln1_b, w_ff1, w_ff2, ln2_g, ln2_b):
    c = D_CONV
    o_q = 3 * c
    o_kv = o_q + Q_LORA
    o_kr = o_kv + KV_LORA
    o_gc = o_kr + QK_ROPE
    o_ga = o_gc + D_MODEL
    d = w_in.shape[0]
    w_in_cg = jnp.concatenate([w_in[:, 0:o_q], w_in[:, o_gc:o_ga + D_MODEL]], axis=1).astype(BF16)
    w_small = jnp.concatenate(
        [w_in[:, o_q:o_gc], jnp.zeros((d, LANES - QK_ROPE), w_in.dtype)], axis=1).astype(BF16)
    wq = w_qb.reshape(Q_LORA, N_HEADS, QK_NOPE + QK_ROPE)
    wq = jnp.concatenate(
        [wq, jnp.zeros((Q_LORA, N_HEADS, HEAD_PAD - QK_NOPE - QK_ROPE), w_qb.dtype)], axis=2)
    return dict(
        w_in_cg=w_in_cg, w_small=w_small, conv_w=conv_w,
        qg=q_norm_g[None, :], kvg=kv_norm_g[None, :],
        w_qb=wq.reshape(Q_LORA, N_HEADS * HEAD_PAD).astype(BF16),
        w_kb=w_kb.reshape(KV_LORA, N_HEADS * QK_NOPE).astype(BF16),
        w_vb_t=w_vb.reshape(KV_LORA, N_HEADS * V_DIM).T.astype(BF16),
        w_kb_heads=jnp.transpose(w_kb, (1, 2, 0)).astype(BF16),
        w_vb_heads=jnp.transpose(w_vb, (1, 0, 2)).astype(BF16),
        w_co=w_conv_out.astype(BF16), w_ao=w_attn_out.astype(BF16), w_mix=w_mix_out.astype(BF16),
        ln1_g=ln1_g[None, :], ln1_b=ln1_b[None, :],
        w_ff1=w_ff1.astype(BF16), w_ff2=w_ff2.astype(BF16),
        ln2_g=ln2_g[None, :], ln2_b=ln2_b[None, :],
    )


def _tail(o, ga, sg, x, lw, alpha):
    h1 = _merge_ln(o, ga, sg, x, lw["w_ao"], lw["w_mix"], lw["ln1_g"], lw["ln1_b"], alpha=alpha)
    return _ffn_ln(h1, lw["w_ff1"], lw["w_ff2"], lw["ln2_g"], lw["ln2_b"], alpha=alpha)


def _prompt_layer(x, conv_prev, lw, alpha):
    b, t, d = x.shape
    ga, sg, conv_new = _conv_gate(x, conv_prev, lw["w_in_cg"], lw["conv_w"], lw["w_co"], sample_rows=None)
    q, k, v_t, lat, krope = _attn_front(x, _rope_tables(jnp.arange(t)), lw["w_small"], lw["qg"], lw["kvg"],
                                        lw["w_qb"], lw["w_kb"], lw["w_vb_t"])
    o = _prompt_attention(q, k, v_t)
    y = _tail(o.reshape(b * t, -1), ga.reshape(b * t, d), sg.reshape(b * t, d), x.reshape(b * t, d), lw, alpha)
    return y.reshape(b, t, d), lat, krope, conv_new


def _sample_layer(x, conv_prev, cache_lat, cache_rope, page_table, lw, alpha):
    s, t, d = x.shape
    past = page_table.shape[1] * PAGE_SIZE
    xt = jnp.transpose(x, (1, 0, 2)).reshape(1, t * s, d)
    state = jnp.transpose(conv_prev, (1, 0, 2)).reshape(1, (CONV_W - 1) * s, D_CONV)
    ga, sg, conv_new = _conv_gate(xt, state, lw["w_in_cg"], lw["conv_w"], lw["w_co"], sample_rows=s)
    tables = _rope_tables(jnp.repeat(past + jnp.arange(t), s))
    q, _, _, lat, krope = _attn_front(xt, tables, lw["w_small"], lw["qg"], lw["kvg"],
                                      lw["w_qb"], lw["w_kb"], lw["w_vb_t"])
    q = q[0]
    q_lat = _head_matmul(q, lw["w_kb_heads"], in_block_stride=HEAD_PAD // QK_NOPE)

    def per_seq(a, w):
        return jnp.transpose(a.reshape(t, s, N_HEADS, w), (1, 0, 2, 3)).reshape(s, t * N_HEADS, w)

    q_rope = per_seq(q.reshape(t * s, N_HEADS, HEAD_PAD)[:, :, QK_NOPE:QK_NOPE + QK_ROPE].reshape(t * s, -1), QK_ROPE)
    lat_seq = jnp.transpose(lat.reshape(t, s, KV_LORA), (1, 0, 2))
    krope_seq = jnp.transpose(krope.reshape(t, s, QK_ROPE), (1, 0, 2))
    pad = ((0, 0), (0, NEW_KEY_PAD - t), (0, 0))
    o_lat = _sample_attention(page_table, per_seq(q_lat, KV_LORA), q_rope,
                              jnp.pad(lat_seq, pad), jnp.pad(krope_seq, pad),
                              cache_lat, jnp.swapaxes(cache_rope, 1, 2))
    o_lat = jnp.transpose(o_lat.reshape(s, t, N_HEADS, KV_LORA), (1, 0, 2, 3)).reshape(t * s, -1)
    o = _head_matmul(o_lat, lw["w_vb_heads"], in_block_stride=1)
    y = _tail(o, ga[0], sg[0], xt[0], lw, alpha)
    y = jnp.transpose(y.reshape(t, s, d), (1, 0, 2))
    conv_new = jnp.transpose(conv_new.reshape(CONV_W - 1, s, D_CONV), (1, 0, 2))
    return y, lat_seq, krope_seq, conv_new


def kernel(x_prompt, x_sample, cache_latent, cache_krope, state_conv, page_table, w_in, conv_w, q_norm_g, w_qb, kv_norm_g, w_kb, w_vb, w_conv_out, w_attn_out, w_mix_out, ln1_g, ln1_b, w_ff1, w_ff2, ln2_g, ln2_b):
    depth = w_in.shape[0]
    alpha = (2 * depth) ** 0.25
    conv_zero = jnp.zeros((x_prompt.shape[0], CONV_W - 1, D_CONV), x_prompt.dtype)
    y_p, y_s = x_prompt, x_sample
    outs = [[] for _ in range(6)]
    for l in range(depth):
        lw = _layer_weights(w_in[l], conv_w[l], q_norm_g[l], w_qb[l], kv_norm_g[l], w_kb[l], w_vb[l],
                            w_conv_out[l], w_attn_out[l], w_mix_out[l], ln1_g[l], ln1_b[l],
                            w_ff1[l], w_ff2[l], ln2_g[l], ln2_b[l])
        y_p, lp, rp, cp = _prompt_layer(y_p, conv_zero, lw, alpha)
        y_s, ls, rs, cs = _sample_layer(y_s, state_conv[l], cache_latent[l], cache_krope[l], page_table, lw, alpha)
        for acc, val in zip(outs, (lp, rp, cp, ls, rs, cs)):
            acc.append(val)
    return (y_p, y_s) + tuple(jnp.stack(o) for o in outs)
```

```python
import functools

import jax
import jax.numpy as jnp
import numpy as np
from jax import lax
from jax.experimental import pallas as pl
from jax.experimental.pallas import tpu as pltpu

F32 = jnp.float32
BF16 = jnp.bfloat16

D_MODEL = 1024
D_CONV = 1024
CONV_W = 3
N_HEADS = 16
QK_NOPE = 128
QK_ROPE = 64
V_DIM = 128
Q_LORA = 256
KV_LORA = 256
ROPE_THETA = 10000.0
D_FF = 4 * D_MODEL
NORM_EPS = 1e-5
PAGE_SIZE = 128
MASK_VALUE = -1e30
ATTN_SCALE = (QK_NOPE + QK_ROPE) ** -0.5

LANES = 128
HEAD_PAD = 2 * LANES
SMALL_W = Q_LORA + KV_LORA + LANES
VMEM_LIMIT = 56 * 1024 * 1024

QK_LOG2_SCALE = ATTN_SCALE * float(np.log2(np.e))

TOKEN_TILE = 512
ATTN_Q_TILE = 512
ATTN_HEADS_PER_STEP = 8
REDUCE_FANIN = 8
PAGES_PER_SLOT = 32
SAMPLE_CHAINS = 4
NEW_KEY_PAD = 16


def _dot(a, b):
    return jnp.dot(a, b, preferred_element_type=F32)


def _dot_nt(a, b):
    return lax.dot_general(a, b, (((1,), (1,)), ((), ())), preferred_element_type=F32)


def _const_spec(shape):
    zeros = (0,) * len(shape)
    return pl.BlockSpec(shape, lambda *_: zeros, pipeline_mode=pl.Buffered(1))


def _params(n_grid_dims, flags=None):
    return pltpu.CompilerParams(
        dimension_semantics=("arbitrary",) * n_grid_dims,
        vmem_limit_bytes=VMEM_LIMIT,
        flags=flags,
    )


def _rms_norm(x, g):
    ms = jnp.mean(jnp.square(x), axis=-1, keepdims=True)
    return x * lax.rsqrt(ms + NORM_EPS) * g


def _layer_norm(x, g, b):
    mu = jnp.mean(x, axis=-1, keepdims=True)
    xc = x - mu
    var = jnp.mean(jnp.square(xc), axis=-1, keepdims=True)
    return xc * lax.rsqrt(var + NORM_EPS) * g + b


def _rope_padded(r, cos, sin_lo, sin_hi):
    return (r * cos
            + pltpu.roll(r, LANES - QK_ROPE // 2, axis=1) * sin_lo
            + pltpu.roll(r, QK_ROPE // 2, axis=1) * sin_hi)


def _conv_gate_kernel(x_ref, state_ref, w_in_ref, conv_w_ref, w_co_ref,
                      ga_ref, sg_ref, conv_new_ref, u_scr, *, sample_rows):
    tm = x_ref.shape[0]
    xb = x_ref[...].astype(BF16)
    c = D_CONV
    u = _dot(xb, w_in_ref[:, c:2 * c]) * _dot(xb, w_in_ref[:, 2 * c:3 * c])
    w0 = conv_w_ref[0:1, :]
    w1 = conv_w_ref[1:2, :]
    w2 = conv_w_ref[2:3, :]
    if sample_rows is None:
        @pl.when(pl.program_id(1) == 0)
        def _():
            u_scr[0:6, :] = jnp.zeros((6, c), F32)
            u_scr[6:8, :] = state_ref[...]
        u_scr[8:8 + tm, :] = u
        v = w2 * u + w1 * u_scr[7:7 + tm, :] + w0 * u_scr[6:6 + tm, :]
        tail = u_scr[tm:tm + 8, :]
        u_scr[0:8, :] = tail
        conv_new_ref[...] = tail[6:8, :]
    else:
        s = sample_rows
        u_scr[0:2 * s, :] = state_ref[...]
        u_scr[2 * s:2 * s + tm, :] = u
        v = w2 * u + w1 * u_scr[s:s + tm, :] + w0 * u_scr[0:tm, :]
        conv_new_ref[...] = u_scr[tm:tm + 2 * s, :]
    a_in = (_dot(xb, w_in_ref[:, 0:c]) * v).astype(BF16)
    a = _dot(a_in, w_co_ref[...])
    ga_ref[...] = jax.nn.sigmoid(_dot(xb, w_in_ref[:, 3 * c:3 * c + D_MODEL])) * a
    sg_ref[...] = jax.nn.sigmoid(_dot(xb, w_in_ref[:, 3 * c + D_MODEL:3 * c + 2 * D_MODEL]))


def _conv_gate(x, state, w_in_cg, conv_w, w_co, *, sample_rows):
    g, t, d = x.shape
    tm = min(TOKEN_TILE, t)
    n_state = state.shape[1]
    scr_rows = 8 + tm if sample_rows is None else 2 * sample_rows + tm
    tok = lambda w: pl.BlockSpec((None, tm, w), lambda b, i: (b, i, 0))
    return pl.pallas_call(
        functools.partial(_conv_gate_kernel, sample_rows=sample_rows),
        grid=(g, t // tm),
        in_specs=[tok(d),
                  pl.BlockSpec((None, n_state, D_CONV), lambda b, i: (b, 0, 0)),
                  _const_spec(w_in_cg.shape), _const_spec(conv_w.shape), _const_spec(w_co.shape)],
        out_specs=[tok(D_MODEL), tok(D_MODEL),
                   pl.BlockSpec((None, n_state, D_CONV), lambda b, i: (b, 0, 0))],
        out_shape=[jax.ShapeDtypeStruct((g, t, D_MODEL), F32),
                   jax.ShapeDtypeStruct((g, t, D_MODEL), F32),
                   jax.ShapeDtypeStruct((g, n_state, D_CONV), F32)],
        scratch_shapes=[pltpu.VMEM((scr_rows, D_CONV), F32)],
        compiler_params=_params(2),
        name="conv_gate",
    )(x, state, w_in_cg, conv_w, w_co)


def _attn_front_kernel(x_ref, cos_ref, sin_lo_ref, sin_hi_ref, w_small_ref, qg_ref, kvg_ref,
                       w_qb_ref, w_kb_ref, w_vb_ref,
                       q_ref, k_ref, vt_ref, lat_ref, krope_ref):
    xb = x_ref[...].astype(BF16)
    small = _dot(xb, w_small_ref[...])
    cos = cos_ref[...]
    sin_lo = sin_lo_ref[...]
    sin_hi = sin_hi_ref[...]

    qn = _rms_norm(small[:, 0:Q_LORA], qg_ref[...]).astype(BF16)
    q = _dot(qn, w_qb_ref[...]) * QK_LOG2_SCALE
    for h in range(N_HEADS):
        lo = h * HEAD_PAD
        q_ref[:, lo:lo + LANES] = q[:, lo:lo + LANES].astype(BF16)
        q_ref[:, lo + LANES:lo + HEAD_PAD] = _rope_padded(
            q[:, lo + LANES:lo + HEAD_PAD], cos, sin_lo, sin_hi).astype(BF16)

    lat = _rms_norm(small[:, Q_LORA:Q_LORA + KV_LORA], kvg_ref[...])
    lat_ref[...] = lat
    latb = lat.astype(BF16)
    krope = _rope_padded(small[:, Q_LORA + KV_LORA:SMALL_W], cos, sin_lo, sin_hi)
    krope_ref[...] = krope[:, 0:QK_ROPE]
    kropeb = krope.astype(BF16)
    kn = _dot(latb, w_kb_ref[...]).astype(BF16)
    for h in range(N_HEADS):
        lo = h * HEAD_PAD
        k_ref[:, lo:lo + LANES] = kn[:, h * QK_NOPE:(h + 1) * QK_NOPE]
        k_ref[:, lo + LANES:lo + HEAD_PAD] = kropeb
    vt_ref[...] = _dot_nt(w_vb_ref[...], latb).astype(BF16)


def _attn_front(x, tables, w_small, qg, kvg, w_qb, w_kb, w_vb_t):
    g, t, d = x.shape
    tm = min(TOKEN_TILE, t)
    tok = lambda w: pl.BlockSpec((None, tm, w), lambda b, i: (b, i, 0))
    tab = pl.BlockSpec((tm, LANES), lambda b, i: (i, 0))
    return pl.pallas_call(
        _attn_front_kernel,
        grid=(g, t // tm),
        in_specs=[tok(d), tab, tab, tab,
                  _const_spec(w_small.shape), _const_spec(qg.shape), _const_spec(kvg.shape),
                  _const_spec(w_qb.shape), _const_spec(w_kb.shape), _const_spec(w_vb_t.shape)],
        out_specs=[tok(N_HEADS * HEAD_PAD), tok(N_HEADS * HEAD_PAD),
                   pl.BlockSpec((None, None, N_HEADS * V_DIM, tm), lambda b, i: (b, i, 0, 0)),
                   tok(KV_LORA), tok(QK_ROPE)],
        out_shape=[jax.ShapeDtypeStruct((g, t, N_HEADS * HEAD_PAD), BF16),
                   jax.ShapeDtypeStruct((g, t, N_HEADS * HEAD_PAD), BF16),
                   jax.ShapeDtypeStruct((g, t // tm, N_HEADS * V_DIM, tm), BF16),
                   jax.ShapeDtypeStruct((g, t, KV_LORA), F32),
                   jax.ShapeDtypeStruct((g, t, QK_ROPE), F32)],
        compiler_params=_params(2),
        name="attn_front",
    )(x, *tables, w_small, qg, kvg, w_qb, w_kb, w_vb_t)


def _softmax_step(s, v, m, l, acc):
    m_new = jnp.maximum(m, jnp.max(s, axis=-1, keepdims=True))
    alpha = jnp.exp2(m - m_new)
    p = jnp.exp2(s - m_new)
    l = alpha * l + jnp.sum(p, axis=-1, keepdims=True)
    acc = alpha * acc + _dot(p.astype(BF16), v)
    return m_new, l, acc


def _softmax_init(rows, width):
    return (jnp.full((rows, 1), MASK_VALUE, F32), jnp.zeros((rows, 1), F32), jnp.zeros((rows, width), F32))


def _prompt_attn_kernel(q_ref, k_ref, vt_ref, o_ref, m_scr, l_scr, acc_scr):
    tq = q_ref.shape[0]
    tk = vt_ref.shape[2]
    qi = pl.program_id(2)
    heads = range(q_ref.shape[1] // HEAD_PAD)

    def reduce_keys(x, op):
        part = op(x.reshape(REDUCE_FANIN, x.shape[0] // REDUCE_FANIN, x.shape[1]), axis=0)
        return op(part, axis=0, keepdims=True)

    def block(j, k_lo=0, k_n=tk, q_lo=0, q_n=tq, diagonal=False):
        start = pl.multiple_of(j * tk, tk) + k_lo
        qcols = slice(q_lo, q_lo + q_n)
        s = [_dot_nt(k_ref[pl.ds(start, k_n), h * HEAD_PAD:(h + 1) * HEAD_PAD],
                     q_ref[qcols, h * HEAD_PAD:(h + 1) * HEAD_PAD]) for h in heads]
        for h in heads:
            s_t = s[h]
            if diagonal:
                key = k_lo + lax.broadcasted_iota(jnp.int32, (k_n, q_n), 0)
                qry = q_lo + lax.broadcasted_iota(jnp.int32, (k_n, q_n), 1)
                s_t = jnp.where(key <= qry, s_t, MASK_VALUE)
            m = m_scr[h, :, qcols]
            m_new = jnp.maximum(m, reduce_keys(s_t, jnp.max))
            alpha = jnp.exp2(m - m_new)
            p_t = jnp.exp2(s_t - m_new)
            m_scr[h, :, qcols] = m_new
            l_scr[h, :, qcols] = alpha * l_scr[h, :, qcols] + reduce_keys(p_t, jnp.sum)
            acc_scr[h, :, qcols] = alpha * acc_scr[h, :, qcols] + _dot(
                vt_ref[j, h * V_DIM:(h + 1) * V_DIM, k_lo:k_lo + k_n], p_t.astype(BF16))

    m_scr[...] = jnp.full(m_scr.shape, MASK_VALUE, F32)
    l_scr[...] = jnp.zeros(l_scr.shape, F32)
    acc_scr[...] = jnp.zeros(acc_scr.shape, F32)

    @pl.loop(0, qi)
    def _(j):
        block(j)

    half = tk // 2
    block(qi, 0, half, 0, tq, diagonal=True)
    block(qi, half, half, half, tq - half, diagonal=True)
    for h in heads:
        o_ref[:, h * V_DIM:(h + 1) * V_DIM] = jnp.transpose(acc_scr[h] / l_scr[h]).astype(o_ref.dtype)


def _prompt_attention(q, k, v_t):
    b, t, _ = q.shape
    tq = ATTN_Q_TILE
    g = ATTN_HEADS_PER_STEP
    n_k, _, tk = v_t.shape[1:]
    assert tk == tq and t == n_k * tk and N_HEADS % g == 0
    return pl.pallas_call(
        _prompt_attn_kernel,
        grid=(b, N_HEADS // g, t // tq),
        in_specs=[pl.BlockSpec((None, tq, g * HEAD_PAD), lambda bi, h, i: (bi, i, h)),
                  pl.BlockSpec((None, t, g * HEAD_PAD), lambda bi, h, i: (bi, 0, h),
                               pipeline_mode=pl.Buffered(1)),
                  pl.BlockSpec((None, n_k, g * V_DIM, tk), lambda bi, h, i: (bi, 0, h, 0),
                               pipeline_mode=pl.Buffered(1))],
        out_specs=pl.BlockSpec((None, tq, g * V_DIM), lambda bi, h, i: (bi, i, h)),
        out_shape=jax.ShapeDtypeStruct((b, t, N_HEADS * V_DIM), BF16),
        scratch_shapes=[pltpu.VMEM((g, 1, tq), F32), pltpu.VMEM((g, 1, tq), F32),
                        pltpu.VMEM((g, V_DIM, tq), F32)],
        compiler_params=_params(3),
        name="prompt_attention",
    )(q, k, v_t)


def _sample_attn_kernel(pt_ref, ql_ref, qr_ref, nl_ref, nr_ref, lat_hbm, ropet_hbm,
                        o_ref, lat_buf, rope_buf, sems, m_scr, l_scr, acc_scr):
    n = PAGES_PER_SLOT
    n_slots = lat_buf.shape[0]
    b = pl.program_id(0)
    last = pl.num_programs(0) - 1
    rows = ql_ref.shape[0]

    def page_copies(seq, slot):
        copies = []
        for j in range(n):
            page = pt_ref[seq, slot * n + j]
            keys = pl.ds(j * PAGE_SIZE, PAGE_SIZE)
            copies.append(pltpu.make_async_copy(lat_hbm.at[page], lat_buf.at[slot, keys, :], sems.at[0, slot]))
            copies.append(pltpu.make_async_copy(ropet_hbm.at[page], rope_buf.at[slot, :, keys], sems.at[1, slot]))
        return copies

    def start_all(copies):
        for i, cp in enumerate(copies):
            cp.start(priority=(i // 2) % 2)

    @pl.when(b == 0)
    def _():
        for slot in range(n_slots):
            start_all(page_copies(b, slot))

    m_scr[...] = jnp.full(m_scr.shape, MASK_VALUE, F32)
    l_scr[...] = jnp.zeros(l_scr.shape, F32)
    acc_scr[...] = jnp.zeros(acc_scr.shape, F32)
    ql = ql_ref[...]
    qr = qr_ref[...]
    span = n * PAGE_SIZE // SAMPLE_CHAINS
    refill = jnp.minimum(b + 1, last)
    for slot in range(n_slots):
        for cp in page_copies(b, slot):
            cp.wait()
        kls = [lat_buf[slot, h * span:(h + 1) * span, :].astype(BF16) for h in range(SAMPLE_CHAINS)]
        scores = [_dot_nt(ql, kls[h]) + _dot(qr, rope_buf[slot, :, h * span:(h + 1) * span].astype(BF16))
                  for h in range(SAMPLE_CHAINS)]
        for h in range(SAMPLE_CHAINS):
            m, l, acc = _softmax_step(scores[h], kls[h], m_scr[h], l_scr[h], acc_scr[h])
            m_scr[h] = m
            l_scr[h] = l
            acc_scr[h] = acc
        start_all(page_copies(refill, slot))

    m, l, acc = m_scr[0], l_scr[0], acc_scr[0]
    for h in range(1, SAMPLE_CHAINS):
        m_new = jnp.maximum(m, m_scr[h])
        wa = jnp.exp2(m - m_new)
        wb = jnp.exp2(m_scr[h] - m_new)
        l = wa * l + wb * l_scr[h]
        acc = wa * acc + wb * acc_scr[h]
        m = m_new
    nl = nl_ref[...].astype(BF16)
    s_new = _dot_nt(ql, nl) + _dot_nt(qr, nr_ref[...].astype(BF16))
    tok = lax.shift_right_logical(lax.broadcasted_iota(jnp.int32, (rows, NEW_KEY_PAD), 0),
                                  int(np.log2(N_HEADS)))
    key = lax.broadcasted_iota(jnp.int32, (rows, NEW_KEY_PAD), 1)
    s_new = jnp.where(key <= tok, s_new, MASK_VALUE)
    _, l2, acc2 = _softmax_step(s_new, nl, m, l, acc)
    o_ref[...] = (acc2 / l2).astype(o_ref.dtype)

    @pl.when(b == last)
    def _():
        for slot in range(n_slots):
            for cp in page_copies(refill, slot):
                cp.wait()


def _sample_attention(page_table, q_lat, q_rope, new_lat, new_rope, cache_lat, cache_rope_t):
    s, rows, _ = q_lat.shape
    n_pages = page_table.shape[1]
    n = PAGES_PER_SLOT
    n_slots = n_pages // n
    assert n_pages == n_slots * n and rows // N_HEADS <= NEW_KEY_PAD
    seq = lambda r, w: pl.BlockSpec((None, r, w), lambda b, pt: (b, 0, 0))
    hbm = pl.BlockSpec(memory_space=pl.ANY)
    grid_spec = pltpu.PrefetchScalarGridSpec(
        num_scalar_prefetch=1,
        grid=(s,),
        in_specs=[seq(rows, KV_LORA), seq(rows, QK_ROPE), seq(NEW_KEY_PAD, KV_LORA), seq(NEW_KEY_PAD, QK_ROPE),
                  hbm, hbm],
        out_specs=seq(rows, KV_LORA),
        scratch_shapes=[pltpu.VMEM((n_slots, n * PAGE_SIZE, KV_LORA), F32),
                        pltpu.VMEM((n_slots, QK_ROPE, n * PAGE_SIZE), F32),
                        pltpu.SemaphoreType.DMA((2, n_slots)),
                        pltpu.VMEM((SAMPLE_CHAINS, rows, 1), F32),
                        pltpu.VMEM((SAMPLE_CHAINS, rows, 1), F32),
                        pltpu.VMEM((SAMPLE_CHAINS, rows, KV_LORA), F32)],
    )
    return pl.pallas_call(
        _sample_attn_kernel,
        grid_spec=grid_spec,
        out_shape=jax.ShapeDtypeStruct((s, rows, KV_LORA), BF16),
        compiler_params=_params(1),
        name="sample_attention",
    )(page_table, q_lat, q_rope, new_lat, new_rope, cache_lat, cache_rope_t)


def _head_matmul_kernel(x_ref, w_ref, o_ref):
    o_ref[...] = _dot(x_ref[...], w_ref[...]).astype(o_ref.dtype)


def _head_matmul(x, w, *, in_block_stride):
    m = x.shape[0]
    h, xw, yw = w.shape
    return pl.pallas_call(
        _head_matmul_kernel,
        grid=(h,),
        in_specs=[pl.BlockSpec((m, xw), lambda i: (0, i * in_block_stride)),
                  pl.BlockSpec((None, xw, yw), lambda i: (i, 0, 0))],
        out_specs=pl.BlockSpec((m, yw), lambda i: (0, i)),
        out_shape=jax.ShapeDtypeStruct((m, h * yw), BF16),
        compiler_params=_params(1),
        name="head_matmul",
    )(x, w)


def _merge_ln_kernel(o_ref, ga_ref, sg_ref, x_ref, w_ao_ref, w_mix_ref, g_ref, b_ref, h1_ref, *, alpha):
    m = _dot(o_ref[...], w_ao_ref[...])
    mix = (ga_ref[...] + sg_ref[...] * m).astype(BF16)
    z = _dot(mix, w_mix_ref[...])
    h1_ref[...] = _layer_norm(alpha * x_ref[...] + z, g_ref[...], b_ref[...])


def _merge_ln(o, ga, sg, x, w_ao, w_mix, g, b, *, alpha):
    t = x.shape[0]
    tm = min(TOKEN_TILE, t)
    tok = lambda w: pl.BlockSpec((tm, w), lambda i: (i, 0))
    return pl.pallas_call(
        functools.partial(_merge_ln_kernel, alpha=alpha),
        grid=(t // tm,),
        in_specs=[tok(o.shape[1]), tok(D_MODEL), tok(D_MODEL), tok(D_MODEL),
                  _const_spec(w_ao.shape), _const_spec(w_mix.shape),
                  _const_spec(g.shape), _const_spec(b.shape)],
        out_specs=tok(D_MODEL),
        out_shape=jax.ShapeDtypeStruct((t, D_MODEL), F32),
        compiler_params=_params(1),
        name="merge_ln",
    )(o, ga, sg, x, w_ao, w_mix, g, b)


FF_CHUNK = 1024


def _ffn_ln_kernel(h1_ref, w1_ref, w2_ref, g_ref, b_ref, y_ref, *, alpha):
    h1 = h1_ref[...]
    hb = h1.astype(BF16)
    f = jnp.zeros(h1.shape, F32)
    for c in range(0, D_FF, FF_CHUNK):
        hid = jnp.square(jax.nn.relu(_dot(hb, w1_ref[:, c:c + FF_CHUNK]))).astype(BF16)
        f = f + _dot(hid, w2_ref[c:c + FF_CHUNK, :])
    y_ref[...] = _layer_norm(alpha * h1 + f, g_ref[...], b_ref[...])


def _ffn_ln(h1, w1, w2, g, b, *, alpha):
    t = h1.shape[0]
    tm = min(TOKEN_TILE, t)
    tok = pl.BlockSpec((tm, D_MODEL), lambda i: (i, 0))
    return pl.pallas_call(
        functools.partial(_ffn_ln_kernel, alpha=alpha),
        grid=(t // tm,),
        in_specs=[tok, _const_spec(w1.shape), _const_spec(w2.shape),
                  _const_spec(g.shape), _const_spec(b.shape)],
        out_specs=tok,
        out_shape=jax.ShapeDtypeStruct((t, D_MODEL), F32),
        compiler_params=_params(1),
        name="ffn_ln",
    )(h1, w1, w2, g, b)


def _rope_tables(pos):
    freqs = ROPE_THETA ** (-jnp.arange(0, QK_ROPE, 2, dtype=F32) / QK_ROPE)
    ang = pos.astype(F32)[:, None] * freqs[None, :]
    c, s = jnp.cos(ang), jnp.sin(ang)
    z = jnp.zeros_like(c)
    return (jnp.concatenate([c, c, z, z], axis=1),
            jnp.concatenate([-s, z, z, z], axis=1),
            jnp.concatenate([z, s, z, z], axis=1))


def _layer_weights(w_in, conv_w, q_norm_g, w_qb, kv_norm_g, w_kb, w_vb, w_conv_out, w_attn_out,
                   w_mix_out, ln1_g, ln1_b, w_ff1, w_ff2, ln2_g, ln2_b):
    c = D_CONV
    o_q = 3 * c
    o_kv = o_q + Q_LORA
    o_kr = o_kv + KV_LORA
    o_gc = o_kr + QK_ROPE
    o_ga = o_gc + D_MODEL
    d = w_in.shape[0]
    w_in_cg = jnp.concatenate([w_in[:, 0:o_q], w_in[:, o_gc:o_ga + D_MODEL]], axis=1).astype(BF16)
    w_small = jnp.concatenate(
        [w_in[:, o_q:o_gc], jnp.zeros((d, LANES - QK_ROPE), w_in.dtype)], axis=1).astype(BF16)
    wq = w_qb.reshape(Q_LORA, N_HEADS, QK_NOPE + QK_ROPE)
    wq = jnp.concatenate(
        [wq, jnp.zeros((Q_LORA, N_HEADS, HEAD_PAD - QK_NOPE - QK_ROPE), w_qb.dtype)], axis=2)
    return dict(
        w_in_cg=w_in_cg, w_small=w_small, conv_w=conv_w,
        qg=q_norm_g[None, :], kvg=kv_norm_g[None, :],
        w_qb=wq.reshape(Q_LORA, N_HEADS * HEAD_PAD).astype(BF16),
        w_kb=w_kb.reshape(KV_LORA, N_HEADS * QK_NOPE).astype(BF16),
        w_vb_t=w_vb.reshape(KV_LORA, N_HEADS * V_DIM).T.astype(BF16),
        w_kb_heads=jnp.transpose(w_kb, (1, 2, 0)).astype(BF16),
        w_vb_heads=jnp.transpose(w_vb, (1, 0, 2)).astype(BF16),
        w_co=w_conv_out.astype(BF16), w_ao=w_attn_out.astype(BF16), w_mix=w_mix_out.astype(BF16),
        ln1_g=ln1_g[None, :], ln1_b=ln1_b[None, :],
        w_ff1=w_ff1.astype(BF16), w_ff2=w_ff2.astype(BF16),
        ln2_g=ln2_g[None, :], ln2_b=ln2_b[None, :],
    )


def _tail(o, ga, sg, x, lw, alpha):
    h1 = _merge_ln(o, ga, sg, x, lw["w_ao"], lw["w_mix"], lw["ln1_g"], lw["ln1_b"], alpha=alpha)
    return _ffn_ln(h1, lw["w_ff1"], lw["w_ff2"], lw["ln2_g"], lw["ln2_b"], alpha=alpha)


def _prompt_layer(x, conv_prev, lw, alpha):
    b, t, d = x.shape
    ga, sg, conv_new = _conv_gate(x, conv_prev, lw["w_in_cg"], lw["conv_w"], lw["w_co"], sample_rows=None)
    q, k, v_t, lat, krope = _attn_front(x, _rope_tables(jnp.arange(t)), lw["w_small"], lw["qg"], lw["kvg"],
                                        lw["w_qb"], lw["w_kb"], lw["w_vb_t"])
    o = _prompt_attention(q, k, v_t)
    y = _tail(o.reshape(b * t, -1), ga.reshape(b * t, d), sg.reshape(b * t, d), x.reshape(b * t, d), lw, alpha)
    return y.reshape(b, t, d), lat, krope, conv_new


def _sample_layer(x, conv_prev, cache_lat, cache_rope, page_table, lw, alpha):
    s, t, d = x.shape
    past = page_table.shape[1] * PAGE_SIZE
    xt = jnp.transpose(x, (1, 0, 2)).reshape(1, t * s, d)
    state = jnp.transpose(conv_prev, (1, 0, 2)).reshape(1, (CONV_W - 1) * s, D_CONV)
    ga, sg, conv_new = _conv_gate(xt, state, lw["w_in_cg"], lw["conv_w"], lw["w_co"], sample_rows=s)
    tables = _rope_tables(jnp.repeat(past + jnp.arange(t), s))
    q, _, _, lat, krope = _attn_front(xt, tables, lw["w_small"], lw["qg"], lw["kvg"],
                                      lw["w_qb"], lw["w_kb"], lw["w_vb_t"])
    q = q[0]
    q_lat = _head_matmul(q, lw["w_kb_heads"], in_block_stride=HEAD_PAD // QK_NOPE)

    def per_seq(a, w):
        return jnp.transpose(a.reshape(t, s, N_HEADS, w), (1, 0, 2, 3)).reshape(s, t * N_HEADS, w)

    q_rope = per_seq(q.reshape(t * s, N_HEADS, HEAD_PAD)[:, :, QK_NOPE:QK_NOPE + QK_ROPE].reshape(t * s, -1), QK_ROPE)
    lat_seq = jnp.transpose(lat.reshape(t, s, KV_LORA), (1, 0, 2))
    krope_seq = jnp.transpose(krope.reshape(t, s, QK_ROPE), (1, 0, 2))
    pad = ((0, 0), (0, NEW_KEY_PAD - t), (0, 0))
    o_lat = _sample_attention(page_table, per_seq(q_lat, KV_LORA), q_rope,
                              jnp.pad(lat_seq, pad), jnp.pad(krope_seq, pad),
                              cache_lat, jnp.swapaxes(cache_rope, 1, 2))
    o_lat = jnp.transpose(o_lat.reshape(s, t, N_HEADS, KV_LORA), (1, 0, 2, 3)).reshape(t * s, -1)
    o = _head_matmul(o_lat, lw["w_vb_heads"], in_block_stride=1)
    y = _tail(o, ga[0], sg[0], xt[0], lw, alpha)
    y = jnp.transpose(y.reshape(t, s, d), (1, 0, 2))
    conv_new = jnp.transpose(conv_new.reshape(CONV_W - 1, s, D_CONV), (1, 0, 2))
    return y, lat_seq, krope_seq, conv_new


def kernel(x_prompt, x_sample, cache_latent, cache_krope, state_conv, page_table, w_in, conv_w, q_norm_g, w_qb, kv_norm_g, w_kb, w_vb, w_conv_out, w_attn_out, w_mix_out, ln1_g, ln1_b, w_ff1, w_ff2, ln2_g, ln2_b):
    depth = w_in.shape[0]
    alpha = (2 * depth) ** 0.25
    conv_zero = jnp.zeros((x_prompt.shape[0], CONV_W - 1, D_CONV), x_prompt.dtype)
    y_p, y_s = x_prompt, x_sample
    outs = [[] for _ in range(6)]
    for l in range(depth):
        lw = _layer_weights(w_in[l], conv_w[l], q_norm_g[l], w_qb[l], kv_norm_g[l], w_kb[l], w_vb[l],
                            w_conv_out[l], w_attn_out[l], w_mix_out[l], ln1_g[l], ln1_b[l],
                            w_ff1[l], w_ff2[l], ln2_g[l], ln2_b[l])
        y_p, lp, rp, cp = _prompt_layer(y_p, conv_zero, lw, alpha)
        y_s, ls, rs, cs = _sample_layer(y_s, state_conv[l], cache_latent[l], cache_krope[l], page_table, lw, alpha)
        for acc, val in zip(outs, (lp, rp, cp, ls, rs, cs)):
            acc.append(val)
    return (y_p, y_s) + tuple(jnp.stack(o) for o in outs)
```

```python
import functools

import jax
import jax.numpy as jnp
import numpy as np
from jax import lax
from jax.experimental import pallas as pl
from jax.experimental.pallas import tpu as pltpu

F32 = jnp.float32
BF16 = jnp.bfloat16

D_MODEL = 1024
D_CONV = 1024
CONV_W = 3
N_HEADS = 16
QK_NOPE = 128
QK_ROPE = 64
V_DIM = 128
Q_LORA = 256
KV_LORA = 256
ROPE_THETA = 10000.0
D_FF = 4 * D_MODEL
NORM_EPS = 1e-5
PAGE_SIZE = 128
MASK_VALUE = -1e30
ATTN_SCALE = (QK_NOPE + QK_ROPE) ** -0.5

LANES = 128
HEAD_PAD = 2 * LANES
SMALL_W = Q_LORA + KV_LORA + LANES
VMEM_LIMIT = 56 * 1024 * 1024

QK_LOG2_SCALE = ATTN_SCALE * float(np.log2(np.e))

TOKEN_TILE = 512
ATTN_Q_TILE = 512
ATTN_HEADS_PER_STEP = 8
REDUCE_FANIN = 8
PAGES_PER_SLOT = 64
SAMPLE_CHAINS = 8
NEW_KEY_PAD = 16


def _dot(a, b):
    return jnp.dot(a, b, preferred_element_type=F32)


def _dot_nt(a, b):
    return lax.dot_general(a, b, (((1,), (1,)), ((), ())), preferred_element_type=F32)


def _const_spec(shape):
    zeros = (0,) * len(shape)
    return pl.BlockSpec(shape, lambda *_: zeros, pipeline_mode=pl.Buffered(1))


def _params(n_grid_dims, flags=None):
    return pltpu.CompilerParams(
        dimension_semantics=("arbitrary",) * n_grid_dims,
        vmem_limit_bytes=VMEM_LIMIT,
        flags=flags,
    )


def _rms_norm(x, g):
    ms = jnp.mean(jnp.square(x), axis=-1, keepdims=True)
    return x * lax.rsqrt(ms + NORM_EPS) * g


def _layer_norm(x, g, b):
    mu = jnp.mean(x, axis=-1, keepdims=True)
    xc = x - mu
    var = jnp.mean(jnp.square(xc), axis=-1, keepdims=True)
    return xc * lax.rsqrt(var + NORM_EPS) * g + b


def _rope_padded(r, cos, sin_lo, sin_hi):
    return (r * cos
            + pltpu.roll(r, LANES - QK_ROPE // 2, axis=1) * sin_lo
            + pltpu.roll(r, QK_ROPE // 2, axis=1) * sin_hi)


def _conv_gate_kernel(x_ref, state_ref, w_in_ref, conv_w_ref, w_co_ref,
                      ga_ref, sg_ref, conv_new_ref, u_scr, *, sample_rows):
    tm = x_ref.shape[0]
    xb = x_ref[...].astype(BF16)
    c = D_CONV
    u = _dot(xb, w_in_ref[:, c:2 * c]) * _dot(xb, w_in_ref[:, 2 * c:3 * c])
    w0 = conv_w_ref[0:1, :]
    w1 = conv_w_ref[1:2, :]
    w2 = conv_w_ref[2:3, :]
    if sample_rows is None:
        @pl.when(pl.program_id(1) == 0)
        def _():
            u_scr[0:6, :] = jnp.zeros((6, c), F32)
            u_scr[6:8, :] = state_ref[...]
        u_scr[8:8 + tm, :] = u
        v = w2 * u + w1 * u_scr[7:7 + tm, :] + w0 * u_scr[6:6 + tm, :]
        tail = u_scr[tm:tm + 8, :]
        u_scr[0:8, :] = tail
        conv_new_ref[...] = tail[6:8, :]
    else:
        s = sample_rows
        u_scr[0:2 * s, :] = state_ref[...]
        u_scr[2 * s:2 * s + tm, :] = u
        v = w2 * u + w1 * u_scr[s:s + tm, :] + w0 * u_scr[0:tm, :]
        conv_new_ref[...] = u_scr[tm:tm + 2 * s, :]
    a_in = (_dot(xb, w_in_ref[:, 0:c]) * v).astype(BF16)
    a = _dot(a_in, w_co_ref[...])
    ga_ref[...] = jax.nn.sigmoid(_dot(xb, w_in_ref[:, 3 * c:3 * c + D_MODEL])) * a
    sg_ref[...] = jax.nn.sigmoid(_dot(xb, w_in_ref[:, 3 * c + D_MODEL:3 * c + 2 * D_MODEL]))


def _conv_gate(x, state, w_in_cg, conv_w, w_co, *, sample_rows):
    g, t, d = x.shape
    tm = min(TOKEN_TILE, t)
    n_state = state.shape[1]
    scr_rows = 8 + tm if sample_rows is None else 2 * sample_rows + tm
    tok = lambda w: pl.BlockSpec((None, tm, w), lambda b, i: (b, i, 0))
    return pl.pallas_call(
        functools.partial(_conv_gate_kernel, sample_rows=sample_rows),
        grid=(g, t // tm),
        in_specs=[tok(d),
                  pl.BlockSpec((None, n_state, D_CONV), lambda b, i: (b, 0, 0)),
                  _const_spec(w_in_cg.shape), _const_spec(conv_w.shape), _const_spec(w_co.shape)],
        out_specs=[tok(D_MODEL), tok(D_MODEL),
                   pl.BlockSpec((None, n_state, D_CONV), lambda b, i: (b, 0, 0))],
        out_shape=[jax.ShapeDtypeStruct((g, t, D_MODEL), F32),
                   jax.ShapeDtypeStruct((g, t, D_MODEL), F32),
                   jax.ShapeDtypeStruct((g, n_state, D_CONV), F32)],
        scratch_shapes=[pltpu.VMEM((scr_rows, D_CONV), F32)],
        compiler_params=_params(2),
        name="conv_gate",
    )(x, state, w_in_cg, conv_w, w_co)


def _attn_front_kernel(x_ref, cos_ref, sin_lo_ref, sin_hi_ref, w_small_ref, qg_ref, kvg_ref,
                       w_qb_ref, w_kb_ref, w_vb_ref,
                       q_ref, k_ref, vt_ref, lat_ref, krope_ref):
    xb = x_ref[...].astype(BF16)
    small = _dot(xb, w_small_ref[...])
    cos = cos_ref[...]
    sin_lo = sin_lo_ref[...]
    sin_hi = sin_hi_ref[...]

    qn = _rms_norm(small[:, 0:Q_LORA], qg_ref[...]).astype(BF16)
    q = _dot(qn, w_qb_ref[...]) * QK_LOG2_SCALE
    for h in range(N_HEADS):
        lo = h * HEAD_PAD
        q_ref[:, lo:lo + LANES] = q[:, lo:lo + LANES].astype(BF16)
        q_ref[:, lo + LANES:lo + HEAD_PAD] = _rope_padded(
            q[:, lo + LANES:lo + HEAD_PAD], cos, sin_lo, sin_hi).astype(BF16)

    lat = _rms_norm(small[:, Q_LORA:Q_LORA + KV_LORA], kvg_ref[...])
    lat_ref[...] = lat
    latb = lat.astype(BF16)
    krope = _rope_padded(small[:, Q_LORA + KV_LORA:SMALL_W], cos, sin_lo, sin_hi)
    krope_ref[...] = krope[:, 0:QK_ROPE]
    kropeb = krope.astype(BF16)
    kn = _dot(latb, w_kb_ref[...]).astype(BF16)
    for h in range(N_HEADS):
        lo = h * HEAD_PAD
        k_ref[:, lo:lo + LANES] = kn[:, h * QK_NOPE:(h + 1) * QK_NOPE]
        k_ref[:, lo + LANES:lo + HEAD_PAD] = kropeb
    vt_ref[...] = _dot_nt(w_vb_ref[...], latb).astype(BF16)


def _attn_front(x, tables, w_small, qg, kvg, w_qb, w_kb, w_vb_t):
    g, t, d = x.shape
    tm = min(TOKEN_TILE, t)
    tok = lambda w: pl.BlockSpec((None, tm, w), lambda b, i: (b, i, 0))
    tab = pl.BlockSpec((tm, LANES), lambda b, i: (i, 0))
    return pl.pallas_call(
        _attn_front_kernel,
        grid=(g, t // tm),
        in_specs=[tok(d), tab, tab, tab,
                  _const_spec(w_small.shape), _const_spec(qg.shape), _const_spec(kvg.shape),
                  _const_spec(w_qb.shape), _const_spec(w_kb.shape), _const_spec(w_vb_t.shape)],
        out_specs=[tok(N_HEADS * HEAD_PAD), tok(N_HEADS * HEAD_PAD),
                   pl.BlockSpec((None, None, N_HEADS * V_DIM, tm), lambda b, i: (b, i, 0, 0)),
                   tok(KV_LORA), tok(QK_ROPE)],
        out_shape=[jax.ShapeDtypeStruct((g, t, N_HEADS * HEAD_PAD), BF16),
                   jax.ShapeDtypeStruct((g, t, N_HEADS * HEAD_PAD), BF16),
                   jax.ShapeDtypeStruct((g, t // tm, N_HEADS * V_DIM, tm), BF16),
                   jax.ShapeDtypeStruct((g, t, KV_LORA), F32),
                   jax.ShapeDtypeStruct((g, t, QK_ROPE), F32)],
        compiler_params=_params(2),
        name="attn_front",
    )(x, *tables, w_small, qg, kvg, w_qb, w_kb, w_vb_t)


def _softmax_step(s, v, m, l, acc):
    m_new = jnp.maximum(m, jnp.max(s, axis=-1, keepdims=True))
    alpha = jnp.exp2(m - m_new)
    p = jnp.exp2(s - m_new)
    l = alpha * l + jnp.sum(p, axis=-1, keepdims=True)
    acc = alpha * acc + _dot(p.astype(BF16), v)
    return m_new, l, acc


def _softmax_init(rows, width):
    return (jnp.full((rows, 1), MASK_VALUE, F32), jnp.zeros((rows, 1), F32), jnp.zeros((rows, width), F32))


def _prompt_attn_kernel(q_ref, k_ref, vt_ref, o_ref, m_scr, l_scr, acc_scr):
    tq = q_ref.shape[0]
    tk = vt_ref.shape[2]
    qi = pl.program_id(2)
    heads = range(q_ref.shape[1] // HEAD_PAD)

    def reduce_keys(x, op):
        part = op(x.reshape(REDUCE_FANIN, x.shape[0] // REDUCE_FANIN, x.shape[1]), axis=0)
        return op(part, axis=0, keepdims=True)

    def block(j, k_lo=0, k_n=tk, q_lo=0, q_n=tq, diagonal=False):
        start = pl.multiple_of(j * tk, tk) + k_lo
        qcols = slice(q_lo, q_lo + q_n)
        s = [_dot_nt(k_ref[pl.ds(start, k_n), h * HEAD_PAD:(h + 1) * HEAD_PAD],
                     q_ref[qcols, h * HEAD_PAD:(h + 1) * HEAD_PAD]) for h in heads]
        for h in heads:
            s_t = s[h]
            if diagonal:
                key = k_lo + lax.broadcasted_iota(jnp.int32, (k_n, q_n), 0)
                qry = q_lo + lax.broadcasted_iota(jnp.int32, (k_n, q_n), 1)
                s_t = jnp.where(key <= qry, s_t, MASK_VALUE)
            m = m_scr[h, :, qcols]
            m_new = jnp.maximum(m, reduce_keys(s_t, jnp.max))
            alpha = jnp.exp2(m - m_new)
            p_t = jnp.exp2(s_t - m_new)
            m_scr[h, :, qcols] = m_new
            l_scr[h, :, qcols] = alpha * l_scr[h, :, qcols] + reduce_keys(p_t, jnp.sum)
            acc_scr[h, :, qcols] = alpha * acc_scr[h, :, qcols] + _dot(
                vt_ref[j, h * V_DIM:(h + 1) * V_DIM, k_lo:k_lo + k_n], p_t.astype(BF16))

    m_scr[...] = jnp.full(m_scr.shape, MASK_VALUE, F32)
    l_scr[...] = jnp.zeros(l_scr.shape, F32)
    acc_scr[...] = jnp.zeros(acc_scr.shape, F32)

    half = tk // 2

    @pl.loop(0, qi)
    def _(j):
        block(j, 0, half)
        block(j, half, half)

    block(qi, 0, half, 0, tq, diagonal=True)
    block(qi, half, half, half, tq - half, diagonal=True)
    for h in heads:
        o_ref[:, h * V_DIM:(h + 1) * V_DIM] = jnp.transpose(acc_scr[h] / l_scr[h]).astype(o_ref.dtype)


def _prompt_attention(q, k, v_t):
    b, t, _ = q.shape
    tq = ATTN_Q_TILE
    g = ATTN_HEADS_PER_STEP
    n_k, _, tk = v_t.shape[1:]
    assert tk == tq and t == n_k * tk and N_HEADS % g == 0
    return pl.pallas_call(
        _prompt_attn_kernel,
        grid=(b, N_HEADS // g, t // tq),
        in_specs=[pl.BlockSpec((None, tq, g * HEAD_PAD), lambda bi, h, i: (bi, i, h)),
                  pl.BlockSpec((None, t, g * HEAD_PAD), lambda bi, h, i: (bi, 0, h),
                               pipeline_mode=pl.Buffered(1)),
                  pl.BlockSpec((None, n_k, g * V_DIM, tk), lambda bi, h, i: (bi, 0, h, 0),
                               pipeline_mode=pl.Buffered(1))],
        out_specs=pl.BlockSpec((None, tq, g * V_DIM), lambda bi, h, i: (bi, i, h)),
        out_shape=jax.ShapeDtypeStruct((b, t, N_HEADS * V_DIM), BF16),
        scratch_shapes=[pltpu.VMEM((g, 1, tq), F32), pltpu.VMEM((g, 1, tq), F32),
                        pltpu.VMEM((g, V_DIM, tq), F32)],
        compiler_params=_params(3),
        name="prompt_attention",
    )(q, k, v_t)


def _sample_attn_kernel(pt_ref, ql_ref, qr_ref, nl_ref, nr_ref, lat_hbm, ropet_hbm,
                        o_ref, lat_buf, rope_buf, sems, m_scr, l_scr, acc_scr):
    n = PAGES_PER_SLOT
    n_slots = lat_buf.shape[0]
    b = pl.program_id(0)
    last = pl.num_programs(0) - 1
    rows = ql_ref.shape[0]

    def page_copies(seq, slot):
        copies = []
        for j in range(n):
            page = pt_ref[seq, slot * n + j]
            keys = pl.ds(j * PAGE_SIZE, PAGE_SIZE)
            copies.append(pltpu.make_async_copy(lat_hbm.at[page], lat_buf.at[slot, keys, :], sems.at[0, slot]))
            copies.append(pltpu.make_async_copy(ropet_hbm.at[page], rope_buf.at[slot, :, keys], sems.at[1, slot]))
        return copies

    def start_all(copies):
        for i, cp in enumerate(copies):
            cp.start(priority=(i // 2) % 2)

    @pl.when(b == 0)
    def _():
        for slot in range(n_slots):
            start_all(page_copies(b, slot))

    m_scr[...] = jnp.full(m_scr.shape, MASK_VALUE, F32)
    l_scr[...] = jnp.zeros(l_scr.shape, F32)
    acc_scr[...] = jnp.zeros(acc_scr.shape, F32)
    ql = ql_ref[...]
    qr = qr_ref[...]
    span = n * PAGE_SIZE // SAMPLE_CHAINS
    refill = jnp.minimum(b + 1, last)
    for slot in range(n_slots):
        for cp in page_copies(b, slot):
            cp.wait()
        kls = [lat_buf[slot, h * span:(h + 1) * span, :].astype(BF16) for h in range(SAMPLE_CHAINS)]
        scores = [_dot_nt(ql, kls[h]) + _dot(qr, rope_buf[slot, :, h * span:(h + 1) * span].astype(BF16))
                  for h in range(SAMPLE_CHAINS)]
        for h in range(SAMPLE_CHAINS):
            m, l, acc = _softmax_step(scores[h], kls[h], m_scr[h], l_scr[h], acc_scr[h])
            m_scr[h] = m
            l_scr[h] = l
            acc_scr[h] = acc
        start_all(page_copies(refill, slot))

    m, l, acc = m_scr[0], l_scr[0], acc_scr[0]
    for h in range(1, SAMPLE_CHAINS):
        m_new = jnp.maximum(m, m_scr[h])
        wa = jnp.exp2(m - m_new)
        wb = jnp.exp2(m_scr[h] - m_new)
        l = wa * l + wb * l_scr[h]
        acc = wa * acc + wb * acc_scr[h]
        m = m_new
    nl = nl_ref[...].astype(BF16)
    s_new = _dot_nt(ql, nl) + _dot_nt(qr, nr_ref[...].astype(BF16))
    tok = lax.shift_right_logical(lax.broadcasted_iota(jnp.int32, (rows, NEW_KEY_PAD), 0),
                                  int(np.log2(N_HEADS)))
    key = lax.broadcasted_iota(jnp.int32, (rows, NEW_KEY_PAD), 1)
    s_new = jnp.where(key <= tok, s_new, MASK_VALUE)
    _, l2, acc2 = _softmax_step(s_new, nl, m, l, acc)
    o_ref[...] = (acc2 / l2).astype(o_ref.dtype)

    @pl.when(b == last)
    def _():
        for slot in range(n_slots):
            for cp in page_copies(refill, slot):
                cp.wait()


def _sample_attention(page_table, q_lat, q_rope, new_lat, new_rope, cache_lat, cache_rope_t):
    s, rows, _ = q_lat.shape
    n_pages = page_table.shape[1]
    n = PAGES_PER_SLOT
    n_slots = n_pages // n
    assert n_pages == n_slots * n and rows // N_HEADS <= NEW_KEY_PAD
    seq = lambda r, w: pl.BlockSpec((None, r, w), lambda b, pt: (b, 0, 0))
    hbm = pl.BlockSpec(memory_space=pl.ANY)
    grid_spec = pltpu.PrefetchScalarGridSpec(
        num_scalar_prefetch=1,
        grid=(s,),
        in_specs=[seq(rows, KV_LORA), seq(rows, QK_ROPE), seq(NEW_KEY_PAD, KV_LORA), seq(NEW_KEY_PAD, QK_ROPE),
                  hbm, hbm],
        out_specs=seq(rows, KV_LORA),
        scratch_shapes=[pltpu.VMEM((n_slots, n * PAGE_SIZE, KV_LORA), F32),
                        pltpu.VMEM((n_slots, QK_ROPE, n * PAGE_SIZE), F32),
                        pltpu.SemaphoreType.DMA((2, n_slots)),
                        pltpu.VMEM((SAMPLE_CHAINS, rows, 1), F32),
                        pltpu.VMEM((SAMPLE_CHAINS, rows, 1), F32),
                        pltpu.VMEM((SAMPLE_CHAINS, rows, KV_LORA), F32)],
    )
    return pl.pallas_call(
        _sample_attn_kernel,
        grid_spec=grid_spec,
        out_shape=jax.ShapeDtypeStruct((s, rows, KV_LORA), BF16),
        compiler_params=_params(1),
        name="sample_attention",
    )(page_table, q_lat, q_rope, new_lat, new_rope, cache_lat, cache_rope_t)


def _head_matmul_kernel(x_ref, w_ref, o_ref):
    o_ref[...] = _dot(x_ref[...], w_ref[...]).astype(o_ref.dtype)


def _head_matmul(x, w, *, in_block_stride):
    m = x.shape[0]
    h, xw, yw = w.shape
    return pl.pallas_call(
        _head_matmul_kernel,
        grid=(h,),
        in_specs=[pl.BlockSpec((m, xw), lambda i: (0, i * in_block_stride)),
                  pl.BlockSpec((None, xw, yw), lambda i: (i, 0, 0))],
        out_specs=pl.BlockSpec((m, yw), lambda i: (0, i)),
        out_shape=jax.ShapeDtypeStruct((m, h * yw), BF16),
        compiler_params=_params(1),
        name="head_matmul",
    )(x, w)


def _merge_ln_kernel(o_ref, ga_ref, sg_ref, x_ref, w_ao_ref, w_mix_ref, g_ref, b_ref, h1_ref, *, alpha):
    m = _dot(o_ref[...], w_ao_ref[...])
    mix = (ga_ref[...] + sg_ref[...] * m).astype(BF16)
    z = _dot(mix, w_mix_ref[...])
    h1_ref[...] = _layer_norm(alpha * x_ref[...] + z, g_ref[...], b_ref[...])


def _merge_ln(o, ga, sg, x, w_ao, w_mix, g, b, *, alpha):
    t = x.shape[0]
    tm = min(TOKEN_TILE, t)
    tok = lambda w: pl.BlockSpec((tm, w), lambda i: (i, 0))
    return pl.pallas_call(
        functools.partial(_merge_ln_kernel, alpha=alpha),
        grid=(t // tm,),
        in_specs=[tok(o.shape[1]), tok(D_MODEL), tok(D_MODEL), tok(D_MODEL),
                  _const_spec(w_ao.shape), _const_spec(w_mix.shape),
                  _const_spec(g.shape), _const_spec(b.shape)],
        out_specs=tok(D_MODEL),
        out_shape=jax.ShapeDtypeStruct((t, D_MODEL), F32),
        compiler_params=_params(1),
        name="merge_ln",
    )(o, ga, sg, x, w_ao, w_mix, g, b)


FF_CHUNK = 1024


def _ffn_ln_kernel(h1_ref, w1_ref, w2_ref, g_ref, b_ref, y_ref, *, alpha):
    h1 = h1_ref[...]
    hb = h1.astype(BF16)
    f = jnp.zeros(h1.shape, F32)
    for c in range(0, D_FF, FF_CHUNK):
        hid = jnp.square(jax.nn.relu(_dot(hb, w1_ref[:, c:c + FF_CHUNK]))).astype(BF16)
        f = f + _dot(hid, w2_ref[c:c + FF_CHUNK, :])
    y_ref[...] = _layer_norm(alpha * h1 + f, g_ref[...], b_ref[...])


def _ffn_ln(h1, w1, w2, g, b, *, alpha):
    t = h1.shape[0]
    tm = min(TOKEN_TILE, t)
    tok = pl.BlockSpec((tm, D_MODEL), lambda i: (i, 0))
    return pl.pallas_call(
        functools.partial(_ffn_ln_kernel, alpha=alpha),
        grid=(t // tm,),
        in_specs=[tok, _const_spec(w1.shape), _const_spec(w2.shape),
                  _const_spec(g.shape), _const_spec(b.shape)],
        out_specs=tok,
        out_shape=jax.ShapeDtypeStruct((t, D_MODEL), F32),
        compiler_params=_params(1),
        name="ffn_ln",
    )(h1, w1, w2, g, b)


def _rope_tables(pos):
    freqs = ROPE_THETA ** (-jnp.arange(0, QK_ROPE, 2, dtype=F32) / QK_ROPE)
    ang = pos.astype(F32)[:, None] * freqs[None, :]
    c, s = jnp.cos(ang), jnp.sin(ang)
    z = jnp.zeros_like(c)
    return (jnp.concatenate([c, c, z, z], axis=1),
            jnp.concatenate([-s, z, z, z], axis=1),
            jnp.concatenate([z, s, z, z], axis=1))


def _layer_weights(w_in, conv_w, q_norm_g, w_qb, kv_norm_g, w_kb, w_vb, w_conv_out, w_attn_out,
                   w_mix_out, ln1_g, ln1_b, w_ff1, w_ff2, ln2_g, ln2_b):
    c = D_CONV
    o_q = 3 * c
    o_kv = o_q + Q_LORA
    o_kr = o_kv + KV_LORA
    o_gc = o_kr + QK_ROPE
    o_ga = o_gc + D_MODEL
    d = w_in.shape[0]
    w_in_cg = jnp.concatenate([w_in[:, 0:o_q], w_in[:, o_gc:o_ga + D_MODEL]], axis=1).astype(BF16)
    w_small = jnp.concatenate(
        [w_in[:, o_q:o_gc], jnp.zeros((d, LANES - QK_ROPE), w_in.dtype)], axis=1).astype(BF16)
    wq = w_qb.reshape(Q_LORA, N_HEADS, QK_NOPE + QK_ROPE)
    wq = jnp.concatenate(
        [wq, jnp.zeros((Q_LORA, N_HEADS, HEAD_PAD - QK_NOPE - QK_ROPE), w_qb.dtype)], axis=2)
    return dict(
        w_in_cg=w_in_cg, w_small=w_small, conv_w=conv_w,
        qg=q_norm_g[None, :], kvg=kv_norm_g[None, :],
        w_qb=wq.reshape(Q_LORA, N_HEADS * HEAD_PAD).astype(BF16),
        w_kb=w_kb.reshape(KV_LORA, N_HEADS * QK_NOPE).astype(BF16),
        w_vb_t=w_vb.reshape(KV_LORA, N_HEADS * V_DIM).T.astype(BF16),
        w_kb_heads=jnp.transpose(w_kb, (1, 2, 0)).astype(BF16),
        w_vb_heads=jnp.transpose(w_vb, (1, 0, 2)).astype(BF16),
        w_co=w_conv_out.astype(BF16), w_ao=w_attn_out.astype(BF16), w_mix=w_mix_out.astype(BF16),
        ln1_g=ln1_g[None, :], ln1_b=ln1_b[None, :],
        w_ff1=w_ff1.astype(BF16), w_ff2=w_ff2.astype(BF16),
        ln2_g=ln2_g[None, :], ln2_b=ln2_b[None, :],
    )


def _tail(o, ga, sg, x, lw, alpha):
    h1 = _merge_ln(o, ga, sg, x, lw["w_ao"], lw["w_mix"], lw["ln1_g"], lw["ln1_b"], alpha=alpha)
    return _ffn_ln(h1, lw["w_ff1"], lw["w_ff2"], lw["ln2_g"], lw["ln2_b"], alpha=alpha)


def _prompt_layer(x, conv_prev, lw, alpha):
    b, t, d = x.shape
    ga, sg, conv_new = _conv_gate(x, conv_prev, lw["w_in_cg"], lw["conv_w"], lw["w_co"], sample_rows=None)
    q, k, v_t, lat, krope = _attn_front(x, _rope_tables(jnp.arange(t)), lw["w_small"], lw["qg"], lw["kvg"],
                                        lw["w_qb"], lw["w_kb"], lw["w_vb_t"])
    o = _prompt_attention(q, k, v_t)
    y = _tail(o.reshape(b * t, -1), ga.reshape(b * t, d), sg.reshape(b * t, d), x.reshape(b * t, d), lw, alpha)
    return y.reshape(b, t, d), lat, krope, conv_new


def _sample_layer(x, conv_prev, cache_lat, cache_rope, page_table, lw, alpha):
    s, t, d = x.shape
    past = page_table.shape[1] * PAGE_SIZE
    xt = jnp.transpose(x, (1, 0, 2)).reshape(1, t * s, d)
    state = jnp.transpose(conv_prev, (1, 0, 2)).reshape(1, (CONV_W - 1) * s, D_CONV)
    ga, sg, conv_new = _conv_gate(xt, state, lw["w_in_cg"], lw["conv_w"], lw["w_co"], sample_rows=s)
    tables = _rope_tables(jnp.repeat(past + jnp.arange(t), s))
    q, _, _, lat, krope = _attn_front(xt, tables, lw["w_small"], lw["qg"], lw["kvg"],
                                      lw["w_qb"], lw["w_kb"], lw["w_vb_t"])
    q = q[0]
    q_lat = _head_matmul(q, lw["w_kb_heads"], in_block_stride=HEAD_PAD // QK_NOPE)

    def per_seq(a, w):
        return jnp.transpose(a.reshape(t, s, N_HEADS, w), (1, 0, 2, 3)).reshape(s, t * N_HEADS, w)

    q_rope = per_seq(q.reshape(t * s, N_HEADS, HEAD_PAD)[:, :, QK_NOPE:QK_NOPE + QK_ROPE].reshape(t * s, -1), QK_ROPE)
    lat_seq = jnp.transpose(lat.reshape(t, s, KV_LORA), (1, 0, 2))
    krope_seq = jnp.transpose(krope.reshape(t, s, QK_ROPE), (1, 0, 2))
    pad = ((0, 0), (0, NEW_KEY_PAD - t), (0, 0))
    o_lat = _sample_attention(page_table, per_seq(q_lat, KV_LORA), q_rope,
                              jnp.pad(lat_seq, pad), jnp.pad(krope_seq, pad),
                              cache_lat, jnp.swapaxes(cache_rope, 1, 2))
    o_lat = jnp.transpose(o_lat.reshape(s, t, N_HEADS, KV_LORA), (1, 0, 2, 3)).reshape(t * s, -1)
    o = _head_matmul(o_lat, lw["w_vb_heads"], in_block_stride=1)
    y = _tail(o, ga[0], sg[0], xt[0], lw, alpha)
    y = jnp.transpose(y.reshape(t, s, d), (1, 0, 2))
    conv_new = jnp.transpose(conv_new.reshape(CONV_W - 1, s, D_CONV), (1, 0, 2))
    return y, lat_seq, krope_seq, conv_new


def kernel(x_prompt, x_sample, cache_latent, cache_krope, state_conv, page_table, w_in, conv_w, q_norm_g, w_qb, kv_norm_g, w_kb, w_vb, w_conv_out, w_attn_out, w_mix_out, ln1_g, ln1_b, w_ff1, w_ff2, ln2_g, ln2_b):
    depth = w_in.shape[0]
    alpha = (2 * depth) ** 0.25
    conv_zero = jnp.zeros((x_prompt.shape[0], CONV_W - 1, D_CONV), x_prompt.dtype)
    y_p, y_s = x_prompt, x_sample
    outs = [[] for _ in range(6)]
    for l in range(depth):
        lw = _layer_weights(w_in[l], conv_w[l], q_norm_g[l], w_qb[l], kv_norm_g[l], w_kb[l], w_vb[l],
                            w_conv_out[l], w_attn_out[l], w_mix_out[l], ln1_g[l], ln1_b[l],
                            w_ff1[l], w_ff2[l], ln2_g[l], ln2_b[l])
        y_p, lp, rp, cp = _prompt_layer(y_p, conv_zero, lw, alpha)
        y_s, ls, rs, cs = _sample_layer(y_s, state_conv[l], cache_latent[l], cache_krope[l], page_table, lw, alpha)
        for acc, val in zip(outs, (lp, rp, cp, ls, rs, cs)):
            acc.append(val)
    return (y_p, y_s) + tuple(jnp.stack(o) for o in outs)
```

```python
import functools

import jax
import jax.numpy as jnp
import numpy as np
from jax import lax
from jax.experimental import pallas as pl
from jax.experimental.pallas import tpu as pltpu

F32 = jnp.float32
BF16 = jnp.bfloat16

D_MODEL = 1024
D_CONV = 1024
CONV_W = 3
N_HEADS = 16
QK_NOPE = 128
QK_ROPE = 64
V_DIM = 128
Q_LORA = 256
KV_LORA = 256
ROPE_THETA = 10000.0
D_FF = 4 * D_MODEL
NORM_EPS = 1e-5
PAGE_SIZE = 128
MASK_VALUE = -1e30
ATTN_SCALE = (QK_NOPE + QK_ROPE) ** -0.5

LANES = 128
HEAD_PAD = 2 * LANES
SMALL_W = Q_LORA + KV_LORA + LANES
VMEM_LIMIT = 56 * 1024 * 1024

QK_LOG2_SCALE = ATTN_SCALE * float(np.log2(np.e))

TOKEN_TILE = 512
ATTN_Q_TILE = 512
ATTN_HEADS_PER_STEP = 8
REDUCE_FANIN = 8
PAGES_PER_SLOT = 32
SAMPLE_CHAINS = 4
NEW_KEY_PAD = 16


def _dot(a, b):
    return jnp.dot(a, b, preferred_element_type=F32)


def _dot_nt(a, b):
    return lax.dot_general(a, b, (((1,), (1,)), ((), ())), preferred_element_type=F32)


def _const_spec(shape):
    zeros = (0,) * len(shape)
    return pl.BlockSpec(shape, lambda *_: zeros, pipeline_mode=pl.Buffered(1))


def _params(n_grid_dims, flags=None):
    return pltpu.CompilerParams(
        dimension_semantics=("arbitrary",) * n_grid_dims,
        vmem_limit_bytes=VMEM_LIMIT,
        flags=flags,
    )


def _rms_norm(x, g):
    ms = jnp.mean(jnp.square(x), axis=-1, keepdims=True)
    return x * lax.rsqrt(ms + NORM_EPS) * g


def _layer_norm(x, g, b):
    mu = jnp.mean(x, axis=-1, keepdims=True)
    xc = x - mu
    var = jnp.mean(jnp.square(xc), axis=-1, keepdims=True)
    return xc * lax.rsqrt(var + NORM_EPS) * g + b


def _rope_padded(r, cos, sin_lo, sin_hi):
    return (r * cos
            + pltpu.roll(r, LANES - QK_ROPE // 2, axis=1) * sin_lo
            + pltpu.roll(r, QK_ROPE // 2, axis=1) * sin_hi)


def _conv_gate_kernel(x_ref, state_ref, w_in_ref, conv_w_ref, w_co_ref,
                      ga_ref, sg_ref, conv_new_ref, u_scr, *, sample_rows):
    tm = x_ref.shape[0]
    xb = x_ref[...].astype(BF16)
    c = D_CONV
    u = _dot(xb, w_in_ref[:, c:2 * c]) * _dot(xb, w_in_ref[:, 2 * c:3 * c])
    w0 = conv_w_ref[0:1, :]
    w1 = conv_w_ref[1:2, :]
    w2 = conv_w_ref[2:3, :]
    if sample_rows is None:
        @pl.when(pl.program_id(1) == 0)
        def _():
            u_scr[0:6, :] = jnp.zeros((6, c), F32)
            u_scr[6:8, :] = state_ref[...]
        u_scr[8:8 + tm, :] = u
        v = w2 * u + w1 * u_scr[7:7 + tm, :] + w0 * u_scr[6:6 + tm, :]
        tail = u_scr[tm:tm + 8, :]
        u_scr[0:8, :] = tail
        conv_new_ref[...] = tail[6:8, :]
    else:
        s = sample_rows
        u_scr[0:2 * s, :] = state_ref[...]
        u_scr[2 * s:2 * s + tm, :] = u
        v = w2 * u + w1 * u_scr[s:s + tm, :] + w0 * u_scr[0:tm, :]
        conv_new_ref[...] = u_scr[tm:tm + 2 * s, :]
    a_in = (_dot(xb, w_in_ref[:, 0:c]) * v).astype(BF16)
    a = _dot(a_in, w_co_ref[...])
    ga_ref[...] = jax.nn.sigmoid(_dot(xb, w_in_ref[:, 3 * c:3 * c + D_MODEL])) * a
    sg_ref[...] = jax.nn.sigmoid(_dot(xb, w_in_ref[:, 3 * c + D_MODEL:3 * c + 2 * D_MODEL]))


def _conv_gate(x, state, w_in_cg, conv_w, w_co, *, sample_rows):
    g, t, d = x.shape
    tm = min(TOKEN_TILE, t)
    n_state = state.shape[1]
    scr_rows = 8 + tm if sample_rows is None else 2 * sample_rows + tm
    tok = lambda w: pl.BlockSpec((None, tm, w), lambda b, i: (b, i, 0))
    return pl.pallas_call(
        functools.partial(_conv_gate_kernel, sample_rows=sample_rows),
        grid=(g, t // tm),
        in_specs=[tok(d),
                  pl.BlockSpec((None, n_state, D_CONV), lambda b, i: (b, 0, 0)),
                  _const_spec(w_in_cg.shape), _const_spec(conv_w.shape), _const_spec(w_co.shape)],
        out_specs=[tok(D_MODEL), tok(D_MODEL),
                   pl.BlockSpec((None, n_state, D_CONV), lambda b, i: (b, 0, 0))],
        out_shape=[jax.ShapeDtypeStruct((g, t, D_MODEL), F32),
                   jax.ShapeDtypeStruct((g, t, D_MODEL), F32),
                   jax.ShapeDtypeStruct((g, n_state, D_CONV), F32)],
        scratch_shapes=[pltpu.VMEM((scr_rows, D_CONV), F32)],
        compiler_params=_params(2),
        name="conv_gate",
    )(x, state, w_in_cg, conv_w, w_co)


def _attn_front_kernel(x_ref, cos_ref, sin_lo_ref, sin_hi_ref, w_small_ref, qg_ref, kvg_ref,
                       w_qb_ref, w_kb_ref, w_vb_ref,
                       q_ref, k_ref, vt_ref, lat_ref, krope_ref):
    xb = x_ref[...].astype(BF16)
    small = _dot(xb, w_small_ref[...])
    cos = cos_ref[...]
    sin_lo = sin_lo_ref[...]
    sin_hi = sin_hi_ref[...]

    qn = _rms_norm(small[:, 0:Q_LORA], qg_ref[...]).astype(BF16)
    q = _dot(qn, w_qb_ref[...]) * QK_LOG2_SCALE
    for h in range(N_HEADS):
        lo = h * HEAD_PAD
        q_ref[:, lo:lo + LANES] = q[:, lo:lo + LANES].astype(BF16)
        q_ref[:, lo + LANES:lo + HEAD_PAD] = _rope_padded(
            q[:, lo + LANES:lo + HEAD_PAD], cos, sin_lo, sin_hi).astype(BF16)

    lat = _rms_norm(small[:, Q_LORA:Q_LORA + KV_LORA], kvg_ref[...])
    lat_ref[...] = lat
    latb = lat.astype(BF16)
    krope = _rope_padded(small[:, Q_LORA + KV_LORA:SMALL_W], cos, sin_lo, sin_hi)
    krope_ref[...] = krope[:, 0:QK_ROPE]
    kropeb = krope.astype(BF16)
    kn = _dot(latb, w_kb_ref[...]).astype(BF16)
    for h in range(N_HEADS):
        lo = h * HEAD_PAD
        k_ref[:, lo:lo + LANES] = kn[:, h * QK_NOPE:(h + 1) * QK_NOPE]
        k_ref[:, lo + LANES:lo + HEAD_PAD] = kropeb
    vt_ref[...] = _dot_nt(w_vb_ref[...], latb).astype(BF16)


def _attn_front(x, tables, w_small, qg, kvg, w_qb, w_kb, w_vb_t):
    g, t, d = x.shape
    tm = min(TOKEN_TILE, t)
    tok = lambda w: pl.BlockSpec((None, tm, w), lambda b, i: (b, i, 0))
    tab = pl.BlockSpec((tm, LANES), lambda b, i: (i, 0))
    return pl.pallas_call(
        _attn_front_kernel,
        grid=(g, t // tm),
        in_specs=[tok(d), tab, tab, tab,
                  _const_spec(w_small.shape), _const_spec(qg.shape), _const_spec(kvg.shape),
                  _const_spec(w_qb.shape), _const_spec(w_kb.shape), _const_spec(w_vb_t.shape)],
        out_specs=[tok(N_HEADS * HEAD_PAD), tok(N_HEADS * HEAD_PAD),
                   pl.BlockSpec((None, None, N_HEADS * V_DIM, tm), lambda b, i: (b, i, 0, 0)),
                   tok(KV_LORA), tok(QK_ROPE)],
        out_shape=[jax.ShapeDtypeStruct((g, t, N_HEADS * HEAD_PAD), BF16),
                   jax.ShapeDtypeStruct((g, t, N_HEADS * HEAD_PAD), BF16),
                   jax.ShapeDtypeStruct((g, t // tm, N_HEADS * V_DIM, tm), BF16),
                   jax.ShapeDtypeStruct((g, t, KV_LORA), F32),
                   jax.ShapeDtypeStruct((g, t, QK_ROPE), F32)],
        compiler_params=_params(2),
        name="attn_front",
    )(x, *tables, w_small, qg, kvg, w_qb, w_kb, w_vb_t)


def _softmax_step(s, v, m, l, acc):
    m_new = jnp.maximum(m, jnp.max(s, axis=-1, keepdims=True))
    alpha = jnp.exp2(m - m_new)
    p = jnp.exp2(s - m_new)
    l = alpha * l + jnp.sum(p, axis=-1, keepdims=True)
    acc = alpha * acc + _dot(p.astype(BF16), v)
    return m_new, l, acc


def _softmax_init(rows, width):
    return (jnp.full((rows, 1), MASK_VALUE, F32), jnp.zeros((rows, 1), F32), jnp.zeros((rows, width), F32))


def _prompt_attn_kernel(q_ref, k_hbm, vt_hbm, o_ref, k_ref, vt_ref, sems, m_scr, l_scr, acc_scr):
    tq = q_ref.shape[0]
    tk = vt_ref.shape[2]
    bi = pl.program_id(0)
    gi = pl.program_id(1)
    qi = pl.program_id(2)
    heads = range(q_ref.shape[1] // HEAD_PAD)

    def chunk_copies(c):
        rows = pl.ds(pl.multiple_of(c * tk, tk), tk)
        kcols = pl.ds(pl.multiple_of(gi * k_ref.shape[1], k_ref.shape[1]), k_ref.shape[1])
        vrows = pl.ds(pl.multiple_of(gi * vt_ref.shape[1], vt_ref.shape[1]), vt_ref.shape[1])
        parity = lax.rem(c, 2)
        return (pltpu.make_async_copy(k_hbm.at[bi, rows, kcols], k_ref.at[rows, :], sems.at[0, parity]),
                pltpu.make_async_copy(vt_hbm.at[bi, c, vrows, :], vt_ref.at[c], sems.at[1, parity]))

    @pl.when(qi == 0)
    def _():
        for cp in chunk_copies(qi):
            cp.start()

    @pl.when(qi + 1 < pl.num_programs(2))
    def _():
        for cp in chunk_copies(qi + 1):
            cp.start()

    for cp in chunk_copies(qi):
        cp.wait()

    def reduce_keys(x, op):
        part = op(x.reshape(REDUCE_FANIN, x.shape[0] // REDUCE_FANIN, x.shape[1]), axis=0)
        return op(part, axis=0, keepdims=True)

    def block(j, k_lo=0, k_n=tk, q_lo=0, q_n=tq, diagonal=False):
        start = pl.multiple_of(j * tk, tk) + k_lo
        qcols = slice(q_lo, q_lo + q_n)
        s = [_dot_nt(k_ref[pl.ds(start, k_n), h * HEAD_PAD:(h + 1) * HEAD_PAD],
                     q_ref[qcols, h * HEAD_PAD:(h + 1) * HEAD_PAD]) for h in heads]
        for h in heads:
            s_t = s[h]
            if diagonal:
                key = k_lo + lax.broadcasted_iota(jnp.int32, (k_n, q_n), 0)
                qry = q_lo + lax.broadcasted_iota(jnp.int32, (k_n, q_n), 1)
                s_t = jnp.where(key <= qry, s_t, MASK_VALUE)
            m = m_scr[h, :, qcols]
            m_new = jnp.maximum(m, reduce_keys(s_t, jnp.max))
            alpha = jnp.exp2(m - m_new)
            p_t = jnp.exp2(s_t - m_new)
            m_scr[h, :, qcols] = m_new
            l_scr[h, :, qcols] = alpha * l_scr[h, :, qcols] + reduce_keys(p_t, jnp.sum)
            acc_scr[h, :, qcols] = alpha * acc_scr[h, :, qcols] + _dot(
                vt_ref[j, h * V_DIM:(h + 1) * V_DIM, k_lo:k_lo + k_n], p_t.astype(BF16))

    m_scr[...] = jnp.full(m_scr.shape, MASK_VALUE, F32)
    l_scr[...] = jnp.zeros(l_scr.shape, F32)
    acc_scr[...] = jnp.zeros(acc_scr.shape, F32)

    @pl.loop(0, qi)
    def _(j):
        block(j)

    half = tk // 2
    block(qi, 0, half, 0, tq, diagonal=True)
    block(qi, half, half, half, tq - half, diagonal=True)
    for h in heads:
        o_ref[:, h * V_DIM:(h + 1) * V_DIM] = jnp.transpose(acc_scr[h] / l_scr[h]).astype(o_ref.dtype)


def _prompt_attention(q, k, v_t):
    b, t, _ = q.shape
    tq = ATTN_Q_TILE
    g = ATTN_HEADS_PER_STEP
    n_k, _, tk = v_t.shape[1:]
    assert tk == tq and t == n_k * tk and N_HEADS % g == 0
    return pl.pallas_call(
        _prompt_attn_kernel,
        grid=(b, N_HEADS // g, t // tq),
        in_specs=[pl.BlockSpec((None, tq, g * HEAD_PAD), lambda bi, h, i: (bi, i, h)),
                  pl.BlockSpec(memory_space=pl.ANY), pl.BlockSpec(memory_space=pl.ANY)],
        out_specs=pl.BlockSpec((None, tq, g * V_DIM), lambda bi, h, i: (bi, i, h)),
        out_shape=jax.ShapeDtypeStruct((b, t, N_HEADS * V_DIM), BF16),
        scratch_shapes=[pltpu.VMEM((t, g * HEAD_PAD), BF16), pltpu.VMEM((n_k, g * V_DIM, tk), BF16),
                        pltpu.SemaphoreType.DMA((2, 2)),
                        pltpu.VMEM((g, 1, tq), F32), pltpu.VMEM((g, 1, tq), F32),
                        pltpu.VMEM((g, V_DIM, tq), F32)],
        compiler_params=_params(3),
        name="prompt_attention",
    )(q, k, v_t)


def _sample_attn_kernel(pt_ref, ql_ref, qr_ref, nl_ref, nr_ref, lat_hbm, ropet_hbm,
                        o_ref, lat_buf, rope_buf, sems, m_scr, l_scr, acc_scr):
    n = PAGES_PER_SLOT
    n_slots = lat_buf.shape[0]
    b = pl.program_id(0)
    last = pl.num_programs(0) - 1
    rows = ql_ref.shape[0]

    def page_copies(seq, slot):
        copies = []
        for j in range(n):
            page = pt_ref[seq, slot * n + j]
            keys = pl.ds(j * PAGE_SIZE, PAGE_SIZE)
            copies.append(pltpu.make_async_copy(lat_hbm.at[page], lat_buf.at[slot, keys, :], sems.at[0, slot]))
            copies.append(pltpu.make_async_copy(ropet_hbm.at[page], rope_buf.at[slot, :, keys], sems.at[1, slot]))
        return copies

    def start_all(copies):
        for i, cp in enumerate(copies):
            cp.start(priority=(i // 2) % 2)

    @pl.when(b == 0)
    def _():
        for slot in range(n_slots):
            start_all(page_copies(b, slot))

    m_scr[...] = jnp.full(m_scr.shape, MASK_VALUE, F32)
    l_scr[...] = jnp.zeros(l_scr.shape, F32)
    acc_scr[...] = jnp.zeros(acc_scr.shape, F32)
    ql = ql_ref[...]
    qr = qr_ref[...]
    span = n * PAGE_SIZE // SAMPLE_CHAINS
    refill = jnp.minimum(b + 1, last)
    for slot in range(n_slots):
        for cp in page_copies(b, slot):
            cp.wait()
        kls = [lat_buf[slot, h * span:(h + 1) * span, :].astype(BF16) for h in range(SAMPLE_CHAINS)]
        scores = [_dot_nt(ql, kls[h]) + _dot(qr, rope_buf[slot, :, h * span:(h + 1) * span].astype(BF16))
                  for h in range(SAMPLE_CHAINS)]
        for h in range(SAMPLE_CHAINS):
            m, l, acc = _softmax_step(scores[h], kls[h], m_scr[h], l_scr[h], acc_scr[h])
            m_scr[h] = m
            l_scr[h] = l
            acc_scr[h] = acc
        start_all(page_copies(refill, slot))

    m, l, acc = m_scr[0], l_scr[0], acc_scr[0]
    for h in range(1, SAMPLE_CHAINS):
        m_new = jnp.maximum(m, m_scr[h])
        wa = jnp.exp2(m - m_new)
        wb = jnp.exp2(m_scr[h] - m_new)
        l = wa * l + wb * l_scr[h]
        acc = wa * acc + wb * acc_scr[h]
        m = m_new
    nl = nl_ref[...].astype(BF16)
    s_new = _dot_nt(ql, nl) + _dot_nt(qr, nr_ref[...].astype(BF16))
    tok = lax.shift_right_logical(lax.broadcasted_iota(jnp.int32, (rows, NEW_KEY_PAD), 0),
                                  int(np.log2(N_HEADS)))
    key = lax.broadcasted_iota(jnp.int32, (rows, NEW_KEY_PAD), 1)
    s_new = jnp.where(key <= tok, s_new, MASK_VALUE)
    _, l2, acc2 = _softmax_step(s_new, nl, m, l, acc)
    o_ref[...] = (acc2 / l2).astype(o_ref.dtype)

    @pl.when(b == last)
    def _():
        for slot in range(n_slots):
            for cp in page_copies(refill, slot):
                cp.wait()


def _sample_attention(page_table, q_lat, q_rope, new_lat, new_rope, cache_lat, cache_rope_t):
    s, rows, _ = q_lat.shape
    n_pages = page_table.shape[1]
    n = PAGES_PER_SLOT
    n_slots = n_pages // n
    assert n_pages == n_slots * n and rows // N_HEADS <= NEW_KEY_PAD
    seq = lambda r, w: pl.BlockSpec((None, r, w), lambda b, pt: (b, 0, 0))
    hbm = pl.BlockSpec(memory_space=pl.ANY)
    grid_spec = pltpu.PrefetchScalarGridSpec(
        num_scalar_prefetch=1,
        grid=(s,),
        in_specs=[seq(rows, KV_LORA), seq(rows, QK_ROPE), seq(NEW_KEY_PAD, KV_LORA), seq(NEW_KEY_PAD, QK_ROPE),
                  hbm, hbm],
        out_specs=seq(rows, KV_LORA),
        scratch_shapes=[pltpu.VMEM((n_slots, n * PAGE_SIZE, KV_LORA), F32),
                        pltpu.VMEM((n_slots, QK_ROPE, n * PAGE_SIZE), F32),
                        pltpu.SemaphoreType.DMA((2, n_slots)),
                        pltpu.VMEM((SAMPLE_CHAINS, rows, 1), F32),
                        pltpu.VMEM((SAMPLE_CHAINS, rows, 1), F32),
                        pltpu.VMEM((SAMPLE_CHAINS, rows, KV_LORA), F32)],
    )
    return pl.pallas_call(
        _sample_attn_kernel,
        grid_spec=grid_spec,
        out_shape=jax.ShapeDtypeStruct((s, rows, KV_LORA), BF16),
        compiler_params=_params(1),
        name="sample_attention",
    )(page_table, q_lat, q_rope, new_lat, new_rope, cache_lat, cache_rope_t)


def _head_matmul_kernel(x_ref, w_ref, o_ref):
    o_ref[...] = _dot(x_ref[...], w_ref[...]).astype(o_ref.dtype)


def _head_matmul(x, w, *, in_block_stride):
    m = x.shape[0]
    h, xw, yw = w.shape
    return pl.pallas_call(
        _head_matmul_kernel,
        grid=(h,),
        in_specs=[pl.BlockSpec((m, xw), lambda i: (0, i * in_block_stride)),
                  pl.BlockSpec((None, xw, yw), lambda i: (i, 0, 0))],
        out_specs=pl.BlockSpec((m, yw), lambda i: (0, i)),
        out_shape=jax.ShapeDtypeStruct((m, h * yw), BF16),
        compiler_params=_params(1),
        name="head_matmul",
    )(x, w)


def _merge_ln_kernel(o_ref, ga_ref, sg_ref, x_ref, w_ao_ref, w_mix_ref, g_ref, b_ref, h1_ref, *, alpha):
    m = _dot(o_ref[...], w_ao_ref[...])
    mix = (ga_ref[...] + sg_ref[...] * m).astype(BF16)
    z = _dot(mix, w_mix_ref[...])
    h1_ref[...] = _layer_norm(alpha * x_ref[...] + z, g_ref[...], b_ref[...])


def _merge_ln(o, ga, sg, x, w_ao, w_mix, g, b, *, alpha):
    t = x.shape[0]
    tm = min(TOKEN_TILE, t)
    tok = lambda w: pl.BlockSpec((tm, w), lambda i: (i, 0))
    return pl.pallas_call(
        functools.partial(_merge_ln_kernel, alpha=alpha),
        grid=(t // tm,),
        in_specs=[tok(o.shape[1]), tok(D_MODEL), tok(D_MODEL), tok(D_MODEL),
                  _const_spec(w_ao.shape), _const_spec(w_mix.shape),
                  _const_spec(g.shape), _const_spec(b.shape)],
        out_specs=tok(D_MODEL),
        out_shape=jax.ShapeDtypeStruct((t, D_MODEL), F32),
        compiler_params=_params(1),
        name="merge_ln",
    )(o, ga, sg, x, w_ao, w_mix, g, b)


FF_CHUNK = 1024


def _ffn_ln_kernel(h1_ref, w1_ref, w2_ref, g_ref, b_ref, y_ref, *, alpha):
    h1 = h1_ref[...]
    hb = h1.astype(BF16)
    f = jnp.zeros(h1.shape, F32)
    for c in range(0, D_FF, FF_CHUNK):
        hid = jnp.square(jax.nn.relu(_dot(hb, w1_ref[:, c:c + FF_CHUNK]))).astype(BF16)
        f = f + _dot(hid, w2_ref[c:c + FF_CHUNK, :])
    y_ref[...] = _layer_norm(alpha * h1 + f, g_ref[...], b_ref[...])


def _ffn_ln(h1, w1, w2, g, b, *, alpha):
    t = h1.shape[0]
    tm = min(TOKEN_TILE, t)
    tok = pl.BlockSpec((tm, D_MODEL), lambda i: (i, 0))
    return pl.pallas_call(
        functools.partial(_ffn_ln_kernel, alpha=alpha),
        grid=(t // tm,),
        in_specs=[tok, _const_spec(w1.shape), _const_spec(w2.shape),
                  _const_spec(g.shape), _const_spec(b.shape)],
        out_specs=tok,
        out_shape=jax.ShapeDtypeStruct((t, D_MODEL), F32),
        compiler_params=_params(1),
        name="ffn_ln",
    )(h1, w1, w2, g, b)


def _rope_tables(pos):
    freqs = ROPE_THETA ** (-jnp.arange(0, QK_ROPE, 2, dtype=F32) / QK_ROPE)
    ang = pos.astype(F32)[:, None] * freqs[None, :]
    c, s = jnp.cos(ang), jnp.sin(ang)
    z = jnp.zeros_like(c)
    return (jnp.concatenate([c, c, z, z], axis=1),
            jnp.concatenate([-s, z, z, z], axis=1),
            jnp.concatenate([z, s, z, z], axis=1))


def _layer_weights(w_in, conv_w, q_norm_g, w_qb, kv_norm_g, w_kb, w_vb, w_conv_out, w_attn_out,
                   w_mix_out, ln1_g, ln1_b, w_ff1, w_ff2, ln2_g, ln2_b):
    c = D_CONV
    o_q = 3 * c
    o_kv = o_q + Q_LORA
    o_kr = o_kv + KV_LORA
    o_gc = o_kr + QK_ROPE
    o_ga = o_gc + D_MODEL
    d = w_in.shape[0]
    w_in_cg = jnp.concatenate([w_in[:, 0:o_q], w_in[:, o_gc:o_ga + D_MODEL]], axis=1).astype(BF16)
    w_small = jnp.concatenate(
        [w_in[:, o_q:o_gc], jnp.zeros((d, LANES - QK_ROPE), w_in.dtype)], axis=1).astype(BF16)
    wq = w_qb.reshape(Q_LORA, N_HEADS, QK_NOPE + QK_ROPE)
    wq = jnp.concatenate(
        [wq, jnp.zeros((Q_LORA, N_HEADS, HEAD_PAD - QK_NOPE - QK_ROPE), w_qb.dtype)], axis=2)
    return dict(
        w_in_cg=w_in_cg, w_small=w_small, conv_w=conv_w,
        qg=q_norm_g[None, :], kvg=kv_norm_g[None, :],
        w_qb=wq.reshape(Q_LORA, N_HEADS * HEAD_PAD).astype(BF16),
        w_kb=w_kb.reshape(KV_LORA, N_HEADS * QK_NOPE).astype(BF16),
        w_vb_t=w_vb.reshape(KV_LORA, N_HEADS * V_DIM).T.astype(BF16),
        w_kb_heads=jnp.transpose(w_kb, (1, 2, 0)).astype(BF16),
        w_vb_heads=jnp.transpose(w_vb, (1, 0, 2)).astype(BF16),
        w_co=w_conv_out.astype(BF16), w_ao=w_attn_out.astype(BF16), w_mix=w_mix_out.astype(BF16),
        ln1_g=ln1_g[None, :], ln1_b=ln1_b[None, :],
        w_ff1=w_ff1.astype(BF16), w_ff2=w_ff2.astype(BF16),
        ln2_g=ln2_g[None, :], ln2_b=ln2_b[None, :],
    )


def _tail(o, ga, sg, x, lw, alpha):
    h1 = _merge_ln(o, ga, sg, x, lw["w_ao"], lw["w_mix"], lw["ln1_g"], lw["ln1_b"], alpha=alpha)
    return _ffn_ln(h1, lw["w_ff1"], lw["w_ff2"], lw["ln2_g"], lw["ln2_b"], alpha=alpha)


def _prompt_layer(x, conv_prev, lw, alpha):
    b, t, d = x.shape
    ga, sg, conv_new = _conv_gate(x, conv_prev, lw["w_in_cg"], lw["conv_w"], lw["w_co"], sample_rows=None)
    q, k, v_t, lat, krope = _attn_front(x, _rope_tables(jnp.arange(t)), lw["w_small"], lw["qg"], lw["kvg"],
                                        lw["w_qb"], lw["w_kb"], lw["w_vb_t"])
    o = _prompt_attention(q, k, v_t)
    y = _tail(o.reshape(b * t, -1), ga.reshape(b * t, d), sg.reshape(b * t, d), x.reshape(b * t, d), lw, alpha)
    return y.reshape(b, t, d), lat, krope, conv_new


def _sample_layer(x, conv_prev, cache_lat, cache_rope, page_table, lw, alpha):
    s, t, d = x.shape
    past = page_table.shape[1] * PAGE_SIZE
    xt = jnp.transpose(x, (1, 0, 2)).reshape(1, t * s, d)
    state = jnp.transpose(conv_prev, (1, 0, 2)).reshape(1, (CONV_W - 1) * s, D_CONV)
    ga, sg, conv_new = _conv_gate(xt, state, lw["w_in_cg"], lw["conv_w"], lw["w_co"], sample_rows=s)
    tables = _rope_tables(jnp.repeat(past + jnp.arange(t), s))
    q, _, _, lat, krope = _attn_front(xt, tables, lw["w_small"], lw["qg"], lw["kvg"],
                                      lw["w_qb"], lw["w_kb"], lw["w_vb_t"])
    q = q[0]
    q_lat = _head_matmul(q, lw["w_kb_heads"], in_block_stride=HEAD_PAD // QK_NOPE)

    def per_seq(a, w):
        return jnp.transpose(a.reshape(t, s, N_HEADS, w), (1, 0, 2, 3)).reshape(s, t * N_HEADS, w)

    q_rope = per_seq(q.reshape(t * s, N_HEADS, HEAD_PAD)[:, :, QK_NOPE:QK_NOPE + QK_ROPE].reshape(t * s, -1), QK_ROPE)
    lat_seq = jnp.transpose(lat.reshape(t, s, KV_LORA), (1, 0, 2))
    krope_seq = jnp.transpose(krope.reshape(t, s, QK_ROPE), (1, 0, 2))
    pad = ((0, 0), (0, NEW_KEY_PAD - t), (0, 0))
    o_lat = _sample_attention(page_table, per_seq(q_lat, KV_LORA), q_rope,
                              jnp.pad(lat_seq, pad), jnp.pad(krope_seq, pad),
                              cache_lat, jnp.swapaxes(cache_rope, 1, 2))
    o_lat = jnp.transpose(o_lat.reshape(s, t, N_HEADS, KV_LORA), (1, 0, 2, 3)).reshape(t * s, -1)
    o = _head_matmul(o_lat, lw["w_vb_heads"], in_block_stride=1)
    y = _tail(o, ga[0], sg[0], xt[0], lw, alpha)
    y = jnp.transpose(y.reshape(t, s, d), (1, 0, 2))
    conv_new = jnp.transpose(conv_new.reshape(CONV_W - 1, s, D_CONV), (1, 0, 2))
    return y, lat_seq, krope_seq, conv_new


def kernel(x_prompt, x_sample, cache_latent, cache_krope, state_conv, page_table, w_in, conv_w, q_norm_g, w_qb, kv_norm_g, w_kb, w_vb, w_conv_out, w_attn_out, w_mix_out, ln1_g, ln1_b, w_ff1, w_ff2, ln2_g, ln2_b):
    depth = w_in.shape[0]
    alpha = (2 * depth) ** 0.25
    conv_zero = jnp.zeros((x_prompt.shape[0], CONV_W - 1, D_CONV), x_prompt.dtype)
    y_p, y_s = x_prompt, x_sample
    outs = [[] for _ in range(6)]
    for l in range(depth):
        lw = _layer_weights(w_in[l], conv_w[l], q_norm_g[l], w_qb[l], kv_norm_g[l], w_kb[l], w_vb[l],
                            w_conv_out[l], w_attn_out[l], w_mix_out[l], ln1_g[l], ln1_b[l],
                            w_ff1[l], w_ff2[l], ln2_g[l], ln2_b[l])
        y_p, lp, rp, cp = _prompt_layer(y_p, conv_zero, lw, alpha)
        y_s, ls, rs, cs = _sample_layer(y_s, state_conv[l], cache_latent[l], cache_krope[l], page_table, lw, alpha)
        for acc, val in zip(outs, (lp, rp, cp, ls, rs, cs)):
            acc.append(val)
    return (y_p, y_s) + tuple(jnp.stack(o) for o in outs)
```

```python
import functools

import jax
import jax.numpy as jnp
import numpy as np
from jax import lax
from jax.experimental import pallas as pl
from jax.experimental.pallas import tpu as pltpu

F32 = jnp.float32
BF16 = jnp.bfloat16

D_MODEL = 1024
D_CONV = 1024
CONV_W = 3
N_HEADS = 16
QK_NOPE = 128
QK_ROPE = 64
V_DIM = 128
Q_LORA = 256
KV_LORA = 256
ROPE_THETA = 10000.0
D_FF = 4 * D_MODEL
NORM_EPS = 1e-5
PAGE_SIZE = 128
MASK_VALUE = -1e30
ATTN_SCALE = (QK_NOPE + QK_ROPE) ** -0.5

LANES = 128
HEAD_PAD = 2 * LANES
SMALL_W = Q_LORA + KV_LORA + LANES
VMEM_LIMIT = 56 * 1024 * 1024

QK_LOG2_SCALE = ATTN_SCALE * float(np.log2(np.e))

TOKEN_TILE = 512
ATTN_Q_TILE = 512
ATTN_HEADS_PER_STEP = 8
REDUCE_FANIN = 8
PAGES_PER_SLOT = 32
SAMPLE_CHAINS = 4
NEW_KEY_PAD = 16


def _dot(a, b):
    return jnp.dot(a, b, preferred_element_type=F32)


def _dot_nt(a, b):
    return lax.dot_general(a, b, (((1,), (1,)), ((), ())), preferred_element_type=F32)


def _const_spec(shape):
    zeros = (0,) * len(shape)
    return pl.BlockSpec(shape, lambda *_: zeros, pipeline_mode=pl.Buffered(1))


def _params(n_grid_dims, flags=None):
    return pltpu.CompilerParams(
        dimension_semantics=("arbitrary",) * n_grid_dims,
        vmem_limit_bytes=VMEM_LIMIT,
        flags=flags,
    )


def _rms_norm(x, g):
    ms = jnp.mean(jnp.square(x), axis=-1, keepdims=True)
    return x * lax.rsqrt(ms + NORM_EPS) * g


def _layer_norm(x, g, b):
    mu = jnp.mean(x, axis=-1, keepdims=True)
    xc = x - mu
    var = jnp.mean(jnp.square(xc), axis=-1, keepdims=True)
    return xc * lax.rsqrt(var + NORM_EPS) * g + b


def _rope_padded(r, cos, sin_lo, sin_hi):
    return (r * cos
            + pltpu.roll(r, LANES - QK_ROPE // 2, axis=1) * sin_lo
            + pltpu.roll(r, QK_ROPE // 2, axis=1) * sin_hi)


def _conv_gate_kernel(x_ref, state_ref, w_in_ref, conv_w_ref, w_co_ref,
                      ga_ref, sg_ref, conv_new_ref, u_scr, *, sample_rows):
    tm = x_ref.shape[0]
    xb = x_ref[...].astype(BF16)
    c = D_CONV
    u = _dot(xb, w_in_ref[:, c:2 * c]) * _dot(xb, w_in_ref[:, 2 * c:3 * c])
    w0 = conv_w_ref[0:1, :]
    w1 = conv_w_ref[1:2, :]
    w2 = conv_w_ref[2:3, :]
    if sample_rows is None:
        @pl.when(pl.program_id(1) == 0)
        def _():
            u_scr[0:6, :] = jnp.zeros((6, c), F32)
            u_scr[6:8, :] = state_ref[...]
        u_scr[8:8 + tm, :] = u
        v = w2 * u + w1 * u_scr[7:7 + tm, :] + w0 * u_scr[6:6 + tm, :]
        tail = u_scr[tm:tm + 8, :]
        u_scr[0:8, :] = tail
        conv_new_ref[...] = tail[6:8, :]
    else:
        s = sample_rows
        u_scr[0:2 * s, :] = state_ref[...]
        u_scr[2 * s:2 * s + tm, :] = u
        v = w2 * u + w1 * u_scr[s:s + tm, :] + w0 * u_scr[0:tm, :]
        conv_new_ref[...] = u_scr[tm:tm + 2 * s, :]
    a_in = (_dot(xb, w_in_ref[:, 0:c]) * v).astype(BF16)
    a = _dot(a_in, w_co_ref[...])
    ga_ref[...] = jax.nn.sigmoid(_dot(xb, w_in_ref[:, 3 * c:3 * c + D_MODEL])) * a
    sg_ref[...] = jax.nn.sigmoid(_dot(xb, w_in_ref[:, 3 * c + D_MODEL:3 * c + 2 * D_MODEL]))


def _conv_gate(x, state, w_in_cg, conv_w, w_co, *, sample_rows):
    g, t, d = x.shape
    tm = min(TOKEN_TILE, t)
    n_state = state.shape[1]
    scr_rows = 8 + tm if sample_rows is None else 2 * sample_rows + tm
    tok = lambda w: pl.BlockSpec((None, tm, w), lambda b, i: (b, i, 0))
    return pl.pallas_call(
        functools.partial(_conv_gate_kernel, sample_rows=sample_rows),
        grid=(g, t // tm),
        in_specs=[tok(d),
                  pl.BlockSpec((None, n_state, D_CONV), lambda b, i: (b, 0, 0)),
                  _const_spec(w_in_cg.shape), _const_spec(conv_w.shape), _const_spec(w_co.shape)],
        out_specs=[tok(D_MODEL), tok(D_MODEL),
                   pl.BlockSpec((None, n_state, D_CONV), lambda b, i: (b, 0, 0))],
        out_shape=[jax.ShapeDtypeStruct((g, t, D_MODEL), F32),
                   jax.ShapeDtypeStruct((g, t, D_MODEL), F32),
                   jax.ShapeDtypeStruct((g, n_state, D_CONV), F32)],
        scratch_shapes=[pltpu.VMEM((scr_rows, D_CONV), F32)],
        compiler_params=_params(2),
        name="conv_gate",
    )(x, state, w_in_cg, conv_w, w_co)


def _attn_front_kernel(x_ref, cos_ref, sin_lo_ref, sin_hi_ref, w_small_ref, qg_ref, kvg_ref,
                       w_qb_ref, w_kb_ref, w_vb_ref,
                       q_ref, k_ref, vt_ref, lat_ref, krope_ref):
    xb = x_ref[...].astype(BF16)
    small = _dot(xb, w_small_ref[...])
    cos = cos_ref[...]
    sin_lo = sin_lo_ref[...]
    sin_hi = sin_hi_ref[...]

    qn = _rms_norm(small[:, 0:Q_LORA], qg_ref[...]).astype(BF16)
    q = _dot(qn, w_qb_ref[...]) * QK_LOG2_SCALE
    for h in range(N_HEADS):
        lo = h * HEAD_PAD
        q_ref[:, lo:lo + LANES] = q[:, lo:lo + LANES].astype(BF16)
        q_ref[:, lo + LANES:lo + HEAD_PAD] = _rope_padded(
            q[:, lo + LANES:lo + HEAD_PAD], cos, sin_lo, sin_hi).astype(BF16)

    lat = _rms_norm(small[:, Q_LORA:Q_LORA + KV_LORA], kvg_ref[...])
    lat_ref[...] = lat
    latb = lat.astype(BF16)
    krope = _rope_padded(small[:, Q_LORA + KV_LORA:SMALL_W], cos, sin_lo, sin_hi)
    krope_ref[...] = krope[:, 0:QK_ROPE]
    kropeb = krope.astype(BF16)
    kn = _dot(latb, w_kb_ref[...]).astype(BF16)
    for h in range(N_HEADS):
        lo = h * HEAD_PAD
        k_ref[:, lo:lo + LANES] = kn[:, h * QK_NOPE:(h + 1) * QK_NOPE]
        k_ref[:, lo + LANES:lo + HEAD_PAD] = kropeb
    vt_ref[...] = _dot_nt(w_vb_ref[...], latb).astype(BF16)


def _attn_front(x, tables, w_small, qg, kvg, w_qb, w_kb, w_vb_t):
    g, t, d = x.shape
    tm = min(TOKEN_TILE, t)
    tok = lambda w: pl.BlockSpec((None, tm, w), lambda b, i: (b, i, 0))
    tab = pl.BlockSpec((tm, LANES), lambda b, i: (i, 0))
    return pl.pallas_call(
        _attn_front_kernel,
        grid=(g, t // tm),
        in_specs=[tok(d), tab, tab, tab,
                  _const_spec(w_small.shape), _const_spec(qg.shape), _const_spec(kvg.shape),
                  _const_spec(w_qb.shape), _const_spec(w_kb.shape), _const_spec(w_vb_t.shape)],
        out_specs=[tok(N_HEADS * HEAD_PAD), tok(N_HEADS * HEAD_PAD),
                   pl.BlockSpec((None, None, N_HEADS * V_DIM, tm), lambda b, i: (b, i, 0, 0)),
                   tok(KV_LORA), tok(QK_ROPE)],
        out_shape=[jax.ShapeDtypeStruct((g, t, N_HEADS * HEAD_PAD), BF16),
                   jax.ShapeDtypeStruct((g, t, N_HEADS * HEAD_PAD), BF16),
                   jax.ShapeDtypeStruct((g, t // tm, N_HEADS * V_DIM, tm), BF16),
                   jax.ShapeDtypeStruct((g, t, KV_LORA), F32),
                   jax.ShapeDtypeStruct((g, t, QK_ROPE), F32)],
        compiler_params=_params(2),
        name="attn_front",
    )(x, *tables, w_small, qg, kvg, w_qb, w_kb, w_vb_t)


def _softmax_step(s, v, m, l, acc):
    m_new = jnp.maximum(m, jnp.max(s, axis=-1, keepdims=True))
    alpha = jnp.exp2(m - m_new)
    p = jnp.exp2(s - m_new)
    l = alpha * l + jnp.sum(p, axis=-1, keepdims=True)
    acc = alpha * acc + _dot(p.astype(BF16), v)
    return m_new, l, acc


def _prompt_attn_kernel(q_ref, k_hbm, vt_hbm, o_ref, k_ref, vt_ref, sems, m_scr, l_scr, acc_scr):
    tq = q_ref.shape[0]
    tk = vt_ref.shape[2]
    bi = pl.program_id(0)
    gi = pl.program_id(1)
    qi = pl.program_id(2)
    heads = range(q_ref.shape[1] // HEAD_PAD)

    def chunk_copies(c):
        rows = pl.ds(pl.multiple_of(c * tk, tk), tk)
        kcols = pl.ds(pl.multiple_of(gi * k_ref.shape[1], k_ref.shape[1]), k_ref.shape[1])
        vrows = pl.ds(pl.multiple_of(gi * vt_ref.shape[1], vt_ref.shape[1]), vt_ref.shape[1])
        parity = lax.rem(c, 2)
        return (pltpu.make_async_copy(k_hbm.at[bi, rows, kcols], k_ref.at[rows, :], sems.at[0, parity]),
                pltpu.make_async_copy(vt_hbm.at[bi, c, vrows, :], vt_ref.at[c], sems.at[1, parity]))

    @pl.when(qi == 0)
    def _():
        for cp in chunk_copies(qi):
            cp.start()

    @pl.when(qi + 1 < pl.num_programs(2))
    def _():
        for cp in chunk_copies(qi + 1):
            cp.start()

    for cp in chunk_copies(qi):
        cp.wait()

    def reduce_keys(x, op):
        part = op(x.reshape(REDUCE_FANIN, x.shape[0] // REDUCE_FANIN, x.shape[1]), axis=0)
        return op(part, axis=0, keepdims=True)

    def block(j, k_lo=0, k_n=tk, q_lo=0, q_n=tq, diagonal=False):
        start = pl.multiple_of(j * tk, tk) + k_lo
        qcols = slice(q_lo, q_lo + q_n)
        s = [_dot_nt(k_ref[pl.ds(start, k_n), h * HEAD_PAD:(h + 1) * HEAD_PAD],
                     q_ref[qcols, h * HEAD_PAD:(h + 1) * HEAD_PAD]) for h in heads]
        for h in heads:
            s_t = s[h]
            if diagonal:
                key = k_lo + lax.broadcasted_iota(jnp.int32, (k_n, q_n), 0)
                qry = q_lo + lax.broadcasted_iota(jnp.int32, (k_n, q_n), 1)
                s_t = jnp.where(key <= qry, s_t, MASK_VALUE)
            m = m_scr[h, :, qcols]
            m_new = jnp.maximum(m, reduce_keys(s_t, jnp.max))
            alpha = jnp.exp2(m - m_new)
            p_t = jnp.exp2(s_t - m_new)
            m_scr[h, :, qcols] = m_new
            l_scr[h, :, qcols] = alpha * l_scr[h, :, qcols] + reduce_keys(p_t, jnp.sum)
            acc_scr[h, :, qcols] = alpha * acc_scr[h, :, qcols] + _dot(
                vt_ref[j, h * V_DIM:(h + 1) * V_DIM, k_lo:k_lo + k_n], p_t.astype(BF16))

    m_scr[...] = jnp.full(m_scr.shape, MASK_VALUE, F32)
    l_scr[...] = jnp.zeros(l_scr.shape, F32)
    acc_scr[...] = jnp.zeros(acc_scr.shape, F32)

    @pl.loop(0, qi)
    def _(j):
        block(j)

    half = tk // 2
    block(qi, 0, half, 0, tq, diagonal=True)
    block(qi, half, half, half, tq - half, diagonal=True)
    for h in heads:
        o_ref[:, h * V_DIM:(h + 1) * V_DIM] = jnp.transpose(acc_scr[h] / l_scr[h]).astype(o_ref.dtype)


def _prompt_attention(q, k, v_t):
    b, t, _ = q.shape
    tq = ATTN_Q_TILE
    g = ATTN_HEADS_PER_STEP
    n_k, _, tk = v_t.shape[1:]
    assert tk == tq and t == n_k * tk and N_HEADS % g == 0
    return pl.pallas_call(
        _prompt_attn_kernel,
        grid=(b, N_HEADS // g, t // tq),
        in_specs=[pl.BlockSpec((None, tq, g * HEAD_PAD), lambda bi, h, i: (bi, i, h)),
                  pl.BlockSpec(memory_space=pl.ANY), pl.BlockSpec(memory_space=pl.ANY)],
        out_specs=pl.BlockSpec((None, tq, g * V_DIM), lambda bi, h, i: (bi, i, h)),
        out_shape=jax.ShapeDtypeStruct((b, t, N_HEADS * V_DIM), BF16),
        scratch_shapes=[pltpu.VMEM((t, g * HEAD_PAD), BF16), pltpu.VMEM((n_k, g * V_DIM, tk), BF16),
                        pltpu.SemaphoreType.DMA((2, 2)),
                        pltpu.VMEM((g, 1, tq), F32), pltpu.VMEM((g, 1, tq), F32),
                        pltpu.VMEM((g, V_DIM, tq), F32)],
        compiler_params=_params(3),
        name="prompt_attention",
    )(q, k, v_t)


def _sample_attn_kernel(pt_ref, ql_ref, qr_ref, nl_ref, nr_ref, lat_hbm, ropet_hbm,
                        o_ref, lat_buf, rope_buf, sems, m_scr, l_scr, acc_scr):
    n = PAGES_PER_SLOT
    n_slots = lat_buf.shape[0]
    b = pl.program_id(0)
    last = pl.num_programs(0) - 1
    rows = ql_ref.shape[0]

    def page_copies(seq, slot):
        copies = []
        for j in range(n):
            page = pt_ref[seq, slot * n + j]
            keys = pl.ds(j * PAGE_SIZE, PAGE_SIZE)
            copies.append(pltpu.make_async_copy(lat_hbm.at[page], lat_buf.at[slot, keys, :], sems.at[0, slot]))
            copies.append(pltpu.make_async_copy(ropet_hbm.at[page], rope_buf.at[slot, :, keys], sems.at[1, slot]))
        return copies

    def start_all(copies):
        for i, cp in enumerate(copies):
            cp.start(priority=(i // 2) % 2)

    @pl.when(b == 0)
    def _():
        for slot in range(n_slots):
            start_all(page_copies(b, slot))

    m_scr[...] = jnp.full(m_scr.shape, MASK_VALUE, F32)
    l_scr[...] = jnp.zeros(l_scr.shape, F32)
    acc_scr[...] = jnp.zeros(acc_scr.shape, F32)
    ql = ql_ref[...]
    qr = qr_ref[...]
    span = n * PAGE_SIZE // SAMPLE_CHAINS
    refill = jnp.minimum(b + 1, last)
    for slot in range(n_slots):
        for cp in page_copies(b, slot):
            cp.wait()
        kls = [lat_buf[slot, h * span:(h + 1) * span, :].astype(BF16) for h in range(SAMPLE_CHAINS)]
        scores = [_dot_nt(ql, kls[h]) + _dot(qr, rope_buf[slot, :, h * span:(h + 1) * span].astype(BF16))
                  for h in range(SAMPLE_CHAINS)]
        for h in range(SAMPLE_CHAINS):
            m, l, acc = _softmax_step(scores[h], kls[h], m_scr[h], l_scr[h], acc_scr[h])
            m_scr[h] = m
            l_scr[h] = l
            acc_scr[h] = acc
        start_all(page_copies(refill, slot))

    m, l, acc = m_scr[0], l_scr[0], acc_scr[0]
    for h in range(1, SAMPLE_CHAINS):
        m_new = jnp.maximum(m, m_scr[h])
        wa = jnp.exp2(m - m_new)
        wb = jnp.exp2(m_scr[h] - m_new)
        l = wa * l + wb * l_scr[h]
        acc = wa * acc + wb * acc_scr[h]
        m = m_new
    nl = nl_ref[...].astype(BF16)
    s_new = _dot_nt(ql, nl) + _dot_nt(qr, nr_ref[...].astype(BF16))
    tok = lax.shift_right_logical(lax.broadcasted_iota(jnp.int32, (rows, NEW_KEY_PAD), 0),
                                  int(np.log2(N_HEADS)))
    key = lax.broadcasted_iota(jnp.int32, (rows, NEW_KEY_PAD), 1)
    s_new = jnp.where(key <= tok, s_new, MASK_VALUE)
    _, l2, acc2 = _softmax_step(s_new, nl, m, l, acc)
    o_ref[...] = (acc2 / l2).astype(o_ref.dtype)

    @pl.when(b == last)
    def _():
        for slot in range(n_slots):
            for cp in page_copies(refill, slot):
                cp.wait()


def _sample_attention(page_table, q_lat, q_rope, new_lat, new_rope, cache_lat, cache_rope_t):
    s, rows, _ = q_lat.shape
    n_pages = page_table.shape[1]
    n = PAGES_PER_SLOT
    n_slots = n_pages // n
    assert n_pages == n_slots * n and rows // N_HEADS <= NEW_KEY_PAD
    seq = lambda r, w: pl.BlockSpec((None, r, w), lambda b, pt: (b, 0, 0))
    hbm = pl.BlockSpec(memory_space=pl.ANY)
    grid_spec = pltpu.PrefetchScalarGridSpec(
        num_scalar_prefetch=1,
        grid=(s,),
        in_specs=[seq(rows, KV_LORA), seq(rows, QK_ROPE), seq(NEW_KEY_PAD, KV_LORA), seq(NEW_KEY_PAD, QK_ROPE),
                  hbm, hbm],
        out_specs=seq(rows, KV_LORA),
        scratch_shapes=[pltpu.VMEM((n_slots, n * PAGE_SIZE, KV_LORA), F32),
                        pltpu.VMEM((n_slots, QK_ROPE, n * PAGE_SIZE), F32),
                        pltpu.SemaphoreType.DMA((2, n_slots)),
                        pltpu.VMEM((SAMPLE_CHAINS, rows, 1), F32),
                        pltpu.VMEM((SAMPLE_CHAINS, rows, 1), F32),
                        pltpu.VMEM((SAMPLE_CHAINS, rows, KV_LORA), F32)],
    )
    return pl.pallas_call(
        _sample_attn_kernel,
        grid_spec=grid_spec,
        out_shape=jax.ShapeDtypeStruct((s, rows, KV_LORA), BF16),
        compiler_params=_params(1),
        name="sample_attention",
    )(page_table, q_lat, q_rope, new_lat, new_rope, cache_lat, cache_rope_t)


def _head_matmul_kernel(x_ref, w_ref, o_ref):
    o_ref[...] = _dot(x_ref[...], w_ref[...]).astype(o_ref.dtype)


def _head_matmul(x, w, *, in_block_stride):
    m = x.shape[0]
    h, xw, yw = w.shape
    return pl.pallas_call(
        _head_matmul_kernel,
        grid=(h,),
        in_specs=[pl.BlockSpec((m, xw), lambda i: (0, i * in_block_stride)),
                  pl.BlockSpec((None, xw, yw), lambda i: (i, 0, 0))],
        out_specs=pl.BlockSpec((m, yw), lambda i: (0, i)),
        out_shape=jax.ShapeDtypeStruct((m, h * yw), BF16),
        compiler_params=_params(1),
        name="head_matmul",
    )(x, w)


def _merge_ln_kernel(o_ref, ga_ref, sg_ref, x_ref, w_ao_ref, w_mix_ref, g_ref, b_ref, h1_ref, *, alpha):
    m = _dot(o_ref[...], w_ao_ref[...])
    mix = (ga_ref[...] + sg_ref[...] * m).astype(BF16)
    z = _dot(mix, w_mix_ref[...])
    h1_ref[...] = _layer_norm(alpha * x_ref[...] + z, g_ref[...], b_ref[...])


def _merge_ln(o, ga, sg, x, w_ao, w_mix, g, b, *, alpha):
    t = x.shape[0]
    tm = min(TOKEN_TILE, t)
    tok = lambda w: pl.BlockSpec((tm, w), lambda i: (i, 0))
    return pl.pallas_call(
        functools.partial(_merge_ln_kernel, alpha=alpha),
        grid=(t // tm,),
        in_specs=[tok(o.shape[1]), tok(D_MODEL), tok(D_MODEL), tok(D_MODEL),
                  _const_spec(w_ao.shape), _const_spec(w_mix.shape),
                  _const_spec(g.shape), _const_spec(b.shape)],
        out_specs=tok(D_MODEL),
        out_shape=jax.ShapeDtypeStruct((t, D_MODEL), F32),
        compiler_params=_params(1),
        name="merge_ln",
    )(o, ga, sg, x, w_ao, w_mix, g, b)


FF_CHUNK = 1024


def _ffn_ln_kernel(h1_ref, w1_ref, w2_ref, g_ref, b_ref, y_ref, *, alpha):
    h1 = h1_ref[...]
    hb = h1.astype(BF16)
    f = jnp.zeros(h1.shape, F32)
    for c in range(0, D_FF, FF_CHUNK):
        hid = jnp.square(jax.nn.relu(_dot(hb, w1_ref[:, c:c + FF_CHUNK]))).astype(BF16)
        f = f + _dot(hid, w2_ref[c:c + FF_CHUNK, :])
    y_ref[...] = _layer_norm(alpha * h1 + f, g_ref[...], b_ref[...])


def _ffn_ln(h1, w1, w2, g, b, *, alpha):
    t = h1.shape[0]
    tm = min(TOKEN_TILE, t)
    tok = pl.BlockSpec((tm, D_MODEL), lambda i: (i, 0))
    return pl.pallas_call(
        functools.partial(_ffn_ln_kernel, alpha=alpha),
        grid=(t // tm,),
        in_specs=[tok, _const_spec(w1.shape), _const_spec(w2.shape),
                  _const_spec(g.shape), _const_spec(b.shape)],
        out_specs=tok,
        out_shape=jax.ShapeDtypeStruct((t, D_MODEL), F32),
        compiler_params=_params(1),
        name="ffn_ln",
    )(h1, w1, w2, g, b)


def _rope_tables(pos):
    freqs = ROPE_THETA ** (-jnp.arange(0, QK_ROPE, 2, dtype=F32) / QK_ROPE)
    ang = pos.astype(F32)[:, None] * freqs[None, :]
    c, s = jnp.cos(ang), jnp.sin(ang)
    z = jnp.zeros_like(c)
    return (jnp.concatenate([c, c, z, z], axis=1),
            jnp.concatenate([-s, z, z, z], axis=1),
            jnp.concatenate([z, s, z, z], axis=1))


def _layer_weights(w_in, conv_w, q_norm_g, w_qb, kv_norm_g, w_kb, w_vb, w_conv_out, w_attn_out,
                   w_mix_out, ln1_g, ln1_b, w_ff1, w_ff2, ln2_g, ln2_b):
    c = D_CONV
    o_q = 3 * c
    o_kv = o_q + Q_LORA
    o_kr = o_kv + KV_LORA
    o_gc = o_kr + QK_ROPE
    o_ga = o_gc + D_MODEL
    d = w_in.shape[0]
    w_in_cg = jnp.concatenate([w_in[:, 0:o_q], w_in[:, o_gc:o_ga + D_MODEL]], axis=1).astype(BF16)
    w_small = jnp.concatenate(
        [w_in[:, o_q:o_gc], jnp.zeros((d, LANES - QK_ROPE), w_in.dtype)], axis=1).astype(BF16)
    wq = w_qb.reshape(Q_LORA, N_HEADS, QK_NOPE + QK_ROPE)
    wq = jnp.concatenate(
        [wq, jnp.zeros((Q_LORA, N_HEADS, HEAD_PAD - QK_NOPE - QK_ROPE), w_qb.dtype)], axis=2)
    return dict(
        w_in_cg=w_in_cg, w_small=w_small, conv_w=conv_w,
        qg=q_norm_g[None, :], kvg=kv_norm_g[None, :],
        w_qb=wq.reshape(Q_LORA, N_HEADS * HEAD_PAD).astype(BF16),
        w_kb=w_kb.reshape(KV_LORA, N_HEADS * QK_NOPE).astype(BF16),
        w_vb_t=w_vb.reshape(KV_LORA, N_HEADS * V_DIM).T.astype(BF16),
        w_kb_heads=jnp.transpose(w_kb, (1, 2, 0)).astype(BF16),
        w_vb_heads=jnp.transpose(w_vb, (1, 0, 2)).astype(BF16),
        w_co=w_conv_out.astype(BF16), w_ao=w_attn_out.astype(BF16), w_mix=w_mix_out.astype(BF16),
        ln1_g=ln1_g[None, :], ln1_b=ln1_b[None, :],
        w_ff1=w_ff1.astype(BF16), w_ff2=w_ff2.astype(BF16),
        ln2_g=ln2_g[None, :], ln2_b=ln2_b[None, :],
    )


def _tail(o, ga, sg, x, lw, alpha):
    h1 = _merge_ln(o, ga, sg, x, lw["w_ao"], lw["w_mix"], lw["ln1_g"], lw["ln1_b"], alpha=alpha)
    return _ffn_ln(h1, lw["w_ff1"], lw["w_ff2"], lw["ln2_g"], lw["ln2_b"], alpha=alpha)


def _prompt_layer(x, conv_prev, lw, alpha):
    b, t, d = x.shape
    ga, sg, conv_new = _conv_gate(x, conv_prev, lw["w_in_cg"], lw["conv_w"], lw["w_co"], sample_rows=None)
    q, k, v_t, lat, krope = _attn_front(x, _rope_tables(jnp.arange(t)), lw["w_small"], lw["qg"], lw["kvg"],
                                        lw["w_qb"], lw["w_kb"], lw["w_vb_t"])
    o = _prompt_attention(q, k, v_t)
    y = _tail(o.reshape(b * t, -1), ga.reshape(b * t, d), sg.reshape(b * t, d), x.reshape(b * t, d), lw, alpha)
    return y.reshape(b, t, d), lat, krope, conv_new


def _sample_layer(x, conv_prev, cache_lat, cache_rope, page_table, lw, alpha):
    s, t, d = x.shape
    past = page_table.shape[1] * PAGE_SIZE
    xt = jnp.transpose(x, (1, 0, 2)).reshape(1, t * s, d)
    state = jnp.transpose(conv_prev, (1, 0, 2)).reshape(1, (CONV_W - 1) * s, D_CONV)
    ga, sg, conv_new = _conv_gate(xt, state, lw["w_in_cg"], lw["conv_w"], lw["w_co"], sample_rows=s)
    tables = _rope_tables(jnp.repeat(past + jnp.arange(t), s))
    q, _, _, lat, krope = _attn_front(xt, tables, lw["w_small"], lw["qg"], lw["kvg"],
                                      lw["w_qb"], lw["w_kb"], lw["w_vb_t"])
    q = q[0]
    q_lat = _head_matmul(q, lw["w_kb_heads"], in_block_stride=HEAD_PAD // QK_NOPE)

    def per_seq(a, w):
        return jnp.transpose(a.reshape(t, s, N_HEADS, w), (1, 0, 2, 3)).reshape(s, t * N_HEADS, w)

    q_rope = per_seq(q.reshape(t * s, N_HEADS, HEAD_PAD)[:, :, QK_NOPE:QK_NOPE + QK_ROPE].reshape(t * s, -1), QK_ROPE)
    lat_seq = jnp.transpose(lat.reshape(t, s, KV_LORA), (1, 0, 2))
    krope_seq = jnp.transpose(krope.reshape(t, s, QK_ROPE), (1, 0, 2))
    pad = ((0, 0), (0, NEW_KEY_PAD - t), (0, 0))
    o_lat = _sample_attention(page_table, per_seq(q_lat, KV_LORA), q_rope,
                              jnp.pad(lat_seq, pad), jnp.pad(krope_seq, pad),
                              cache_lat, jnp.swapaxes(cache_rope, 1, 2))
    o_lat = jnp.transpose(o_lat.reshape(s, t, N_HEADS, KV_LORA), (1, 0, 2, 3)).reshape(t * s, -1)
    o = _head_matmul(o_lat, lw["w_vb_heads"], in_block_stride=1)
    y = _tail(o, ga[0], sg[0], xt[0], lw, alpha)
    y = jnp.transpose(y.reshape(t, s, d), (1, 0, 2))
    conv_new = jnp.transpose(conv_new.reshape(CONV_W - 1, s, D_CONV), (1, 0, 2))
    return y, lat_seq, krope_seq, conv_new


def kernel(x_prompt, x_sample, cache_latent, cache_krope, state_conv, page_table, w_in, conv_w, q_norm_g, w_qb, kv_norm_g, w_kb, w_vb, w_conv_out, w_attn_out, w_mix_out, ln1_g, ln1_b, w_ff1, w_ff2, ln2_g, ln2_b):
    depth = w_in.shape[0]
    alpha = (2 * depth) ** 0.25
    conv_zero = jnp.zeros((x_prompt.shape[0], CONV_W - 1, D_CONV), x_prompt.dtype)
    y_p, y_s = x_prompt, x_sample
    outs = [[] for _ in range(6)]
    for l in range(depth):
        lw = _layer_weights(w_in[l], conv_w[l], q_norm_g[l], w_qb[l], kv_norm_g[l], w_kb[l], w_vb[l],
                            w_conv_out[l], w_attn_out[l], w_mix_out[l], ln1_g[l], ln1_b[l],
                            w_ff1[l], w_ff2[l], ln2_g[l], ln2_b[l])
        y_p, lp, rp, cp = _prompt_layer(y_p, conv_zero, lw, alpha)
        y_s, ls, rs, cs = _sample_layer(y_s, state_conv[l], cache_latent[l], cache_krope[l], page_table, lw, alpha)
        for acc, val in zip(outs, (lp, rp, cp, ls, rs, cs)):
            acc.append(val)
    return (y_p, y_s) + tuple(jnp.stack(o) for o in outs)
```
